```python
import math, functools
import jax, jax.numpy as jnp
from jax import lax
import numpy as np

D_MODEL = 2048
BATCH = 2
SEQ = 16384
DEPTH = 2

GRID_W = 64
CTX_LEN = 256
N_MIXERS = 2
EPS = 1e-6

SC_CONV_W = 3

D_INNER = 2 * D_MODEL
HEAD_DIM = 64
SSD_HEADS = D_INNER // HEAD_DIM
D_STATE = 128
SSD_GROUPS = 8
HEADS_PER_GROUP = SSD_HEADS // SSD_GROUPS
SSD_CONV_W = 5
CHUNK = 128
CONV_DIM = D_INNER + 2 * SSD_GROUPS * D_STATE
D_IN_PROJ = D_INNER + CONV_DIM + 2 * SSD_HEADS
DT_MIN = 0.001
DT_MAX = 0.1

N_GROUPS = 4
EXPERTS_PER_GROUP = 8
N_EXPERTS = N_GROUPS * EXPERTS_PER_GROUP
TOP_K = 2
D_EXPERT = D_MODEL // 4

kernel_name = 'hybrid_shortconv_ssd_hmoe_prefix'


def rms_norm(x, w):
    xf = x.astype(jnp.float32)
    y = xf * lax.rsqrt(jnp.mean(xf * xf, axis=-1, keepdims=True) + EPS)
    return (y * w.astype(jnp.float32)).astype(x.dtype)


def modulate(x, w, shift, scale):
    return rms_norm(x, w) * (1 + scale) + shift


def dwconv_seq(u, w, b=None):
    width = w.shape[0]
    y = lax.conv_general_dilated(u, w[:, None, :].astype(u.dtype), window_strides=(1,),
                                 padding=[(width // 2, width // 2)],
                                 dimension_numbers=('NWC', 'WIO', 'NWC'),
                                 feature_group_count=u.shape[-1])
    return y if b is None else y + b.astype(u.dtype)


def dwconv_grid(u, w, b=None, *, rows):
    bsz, n_tok, ch = u.shape
    y = dwconv_seq(u.reshape(bsz * rows, GRID_W, ch), w, b)
    return y.reshape(bsz, n_tok, ch)


def short_conv_mixer(h, w_in, conv_w, w_out, conv_fn):
    gate_b, gate_c, v = jnp.split(h @ w_in, 3, axis=-1)
    return (gate_b * conv_fn(gate_c * v, conv_w)) @ w_out


def ssd_scan(xs, bm, cm, dt, a, h0):
    bsz, seq_len = xs.shape[0], xs.shape[1]
    n_chunks = seq_len // CHUNK
    xg = xs.reshape(bsz, seq_len, SSD_GROUPS, HEADS_PER_GROUP, HEAD_DIM)
    dtg = dt.reshape(bsz, seq_len, SSD_GROUPS, HEADS_PER_GROUP)
    ag = a.reshape(SSD_GROUPS, HEADS_PER_GROUP)
    causal = jnp.tril(jnp.ones((CHUNK, CHUNK), dtype=bool))[None, :, :, None, None]

    def chunks(t):
        return jnp.moveaxis(t.reshape(bsz, n_chunks, CHUNK, *t.shape[2:]), 1, 0)

    def step(h, inp):
        x_c, b_c, c_c, dt_c = inp
        cum = jnp.cumsum(dt_c * ag, axis=1)
        seg = jnp.exp(jnp.where(causal, cum[:, :, None] - cum[:, None, :], -jnp.inf))
        w = jnp.einsum('btgn,bsgn->btsg', c_c, b_c)[..., None] * seg * dt_c[:, None]
        y = jnp.einsum('btsgh,bsghp->btghp', w, x_c)
        y = y + jnp.einsum('btgn,bghpn->btghp', c_c, h) * jnp.exp(cum)[..., None]
        to_end = jnp.exp(cum[:, -1:] - cum) * dt_c
        h = jnp.exp(cum[:, -1])[..., None, None] * h + jnp.einsum('bsgh,bsgn,bsghp->bghpn', to_end, b_c, x_c)
        return h, y

    h_last, ys = lax.scan(step, h0, (chunks(xg), chunks(bm), chunks(cm), chunks(dtg)))
    y = jnp.moveaxis(ys, 0, 1).reshape(bsz, seq_len, D_INNER)
    return y, h_last


def ssd_stream(h, w_in, conv_w, conv_b, dt_bias, a_log, d_skip, conv_fn, h0_f, h0_b):
    bsz, seq_len, _ = h.shape
    zxbcdt = h @ w_in
    z = zxbcdt[..., :D_INNER]
    xbc = jax.nn.silu(conv_fn(zxbcdt[..., D_INNER:D_INNER + CONV_DIM], conv_w, conv_b))
    dt_raw = zxbcdt[..., D_INNER + CONV_DIM:].reshape(bsz, seq_len, 2, SSD_HEADS)
    gn = SSD_GROUPS * D_STATE
    xs = xbc[..., :D_INNER].reshape(bsz, seq_len, SSD_HEADS, HEAD_DIM).astype(jnp.float32)
    bm = xbc[..., D_INNER:D_INNER + gn].reshape(bsz, seq_len, SSD_GROUPS, D_STATE).astype(jnp.float32)
    cm = xbc[..., D_INNER + gn:].reshape(bsz, seq_len, SSD_GROUPS, D_STATE).astype(jnp.float32)
    dt = jax.nn.softplus(dt_raw.astype(jnp.float32) + dt_bias.astype(jnp.float32))
    a = -jnp.exp(a_log.astype(jnp.float32))
    flip = lambda t: jnp.flip(t, axis=1)
    y_f, h_f = ssd_scan(xs, bm, cm, dt[:, :, 0], a[0], h0_f)
    y_b, h_b = ssd_scan(flip(xs), flip(bm), flip(cm), flip(dt[:, :, 1]), a[1], h0_b)
    skip = (d_skip.astype(jnp.float32)[:, None] * xs).reshape(bsz, seq_len, D_INNER)
    return y_f + flip(y_b) + skip, z, h_f, h_b


def ssd_out(y, z, norm_w, w_out):
    g = y * jax.nn.silu(z.astype(jnp.float32))
    gs = g.reshape(*g.shape[:-1], SSD_GROUPS, D_INNER // SSD_GROUPS)
    gs = gs * lax.rsqrt(jnp.mean(gs * gs, axis=-1, keepdims=True) + EPS)
    return (gs.reshape(g.shape) * norm_w.astype(jnp.float32)).astype(z.dtype) @ w_out


def hier_moe(h, rg_w, rg_b, re_w, re_b, w_gate, w_up, w_down):
    shape = h.shape
    t = h.reshape(-1, shape[-1])
    n_tok = t.shape[0]
    tok = jnp.arange(n_tok)
    g_logits = (t @ rg_w + rg_b).astype(jnp.float32)
    g_sel = jnp.argmax(g_logits, axis=-1)
    p_group = jax.nn.softmax(g_logits, axis=-1)[tok, g_sel][:, None]
    e_logits = (t @ re_w + re_b).astype(jnp.float32).reshape(n_tok, N_GROUPS, EXPERTS_PER_GROUP)
    e_in = e_logits[tok, g_sel]
    top_p, top_i = lax.top_k(jax.nn.softmax(e_in, axis=-1), TOP_K)
    top_w = top_p / jnp.sum(top_p, axis=-1, keepdims=True) * p_group
    ids = g_sel[:, None] * EXPERTS_PER_GROUP + top_i
    combine = jnp.einsum('nk,nke->ne', top_w,
                         jax.nn.one_hot(ids, N_EXPERTS, dtype=jnp.float32)).astype(t.dtype)
    out = jnp.zeros_like(t)
    for e in range(N_EXPERTS):
        hid = jax.nn.silu(t @ w_gate[e]) * (t @ w_up[e])
        out = out + combine[:, e:e + 1] * (hid @ w_down[e])
    return out.reshape(shape)


def setup_inputs(seed: int = 0) -> dict:
    key = jax.random.key(seed)
    ks = jax.random.split(key, 28)
    d = D_MODEL
    n_conv = (DEPTH + N_MIXERS - 1) // N_MIXERS
    n_ssd = (DEPTH + N_MIXERS - 2) // N_MIXERS

    def nrm(k, shape, scale):
        return jax.random.normal(k, shape, jnp.float32) * scale

    dt0 = jnp.exp(jax.random.uniform(ks[14], (n_ssd, 2, SSD_HEADS), jnp.float32,
                                     minval=math.log(DT_MIN), maxval=math.log(DT_MAX)))
    return {
        'x': nrm(ks[0], (BATCH, SEQ, d), 1.0),
        'c': nrm(ks[1], (BATCH, d), 1.0),
        'ctx': nrm(ks[2], (BATCH, CTX_LEN, d), 1.0),
        'c_ctx': nrm(ks[3], (d,), 1.0),
        'ada_w': nrm(ks[4], (DEPTH, d, 6 * d), 0.5 * d ** -0.5),
        'ada_b': nrm(ks[5], (DEPTH, 6 * d), 0.02),
        'norm_mix_w': 1.0 + nrm(ks[6], (DEPTH, d), 0.02),
        'norm_ffn_w': 1.0 + nrm(ks[7], (DEPTH, d), 0.02),
        'sc_w_in': nrm(ks[8], (n_conv, d, 3 * d), d ** -0.5),
        'sc_conv_w': nrm(ks[9], (n_conv, SC_CONV_W, d), SC_CONV_W ** -0.5),
        'sc_w_out': nrm(ks[10], (n_conv, d, d), d ** -0.5),
        'ssd_w_in': nrm(ks[11], (n_ssd, d, D_IN_PROJ), d ** -0.5),
        'ssd_conv_w': nrm(ks[12], (n_ssd, SSD_CONV_W, CONV_DIM), SSD_CONV_W ** -0.5),
        'ssd_conv_b': nrm(ks[13], (n_ssd, CONV_DIM), 0.02),
        'ssd_dt_bias': dt0 + jnp.log(-jnp.expm1(-dt0)),
        'ssd_a_log': jnp.log(jax.random.uniform(ks[15], (n_ssd, 2, SSD_HEADS), jnp.float32, minval=1.0, maxval=16.0)),
        'ssd_d': 1.0 + nrm(ks[16], (n_ssd, SSD_HEADS), 0.1),
        'ssd_norm_w': 1.0 + nrm(ks[17], (n_ssd, D_INNER), 0.02),
        'ssd_w_out': nrm(ks[18], (n_ssd, D_INNER, d), D_INNER ** -0.5),
        'rg_w': nrm(ks[19], (DEPTH, d, N_GROUPS), d ** -0.5),
        'rg_b': nrm(ks[20], (DEPTH, N_GROUPS), 0.01),
        're_w': nrm(ks[21], (DEPTH, d, N_EXPERTS), d ** -0.5),
        're_b': nrm(ks[22], (DEPTH, N_EXPERTS), 0.01),
        'moe_w_gate': nrm(ks[23], (DEPTH, N_EXPERTS, d, D_EXPERT), d ** -0.5),
        'moe_w_up': nrm(ks[24], (DEPTH, N_EXPERTS, d, D_EXPERT), d ** -0.5),
        'moe_w_down': nrm(ks[25], (DEPTH, N_EXPERTS, D_EXPERT, d), D_EXPERT ** -0.5),
        'final_norm_w': 1.0 + nrm(ks[26], (d,), 0.02),
    }


def reference(x, c, ctx, c_ctx, ada_w, ada_b, norm_mix_w, norm_ffn_w,
              sc_w_in, sc_conv_w, sc_w_out,
              ssd_w_in, ssd_conv_w, ssd_conv_b, ssd_dt_bias, ssd_a_log, ssd_d, ssd_norm_w, ssd_w_out,
              rg_w, rg_b, re_w, re_b, moe_w_gate, moe_w_up, moe_w_down, final_norm_w):
    bsz, n_lat, _ = x.shape
    rows = n_lat // GRID_W
    grid_conv = functools.partial(dwconv_grid, rows=rows)
    lat, cx = x, ctx
    s_lat = jax.nn.silu(c)[:, None, :]
    s_ctx = jax.nn.silu(c_ctx)[None, None, :]
    for i in range(DEPTH):
        last = i == DEPTH - 1
        j = i // N_MIXERS
        use_ssd = i % N_MIXERS == 1
        ctx_needed = (not last) or use_ssd
        m_lat = jnp.split(s_lat @ ada_w[i] + ada_b[i], 6, axis=-1)
        h_lat = modulate(lat, norm_mix_w[i], m_lat[0], m_lat[1])
        if ctx_needed:
            m_ctx = jnp.split(s_ctx @ ada_w[i] + ada_b[i], 6, axis=-1)
            h_ctx = modulate(cx, norm_mix_w[i], m_ctx[0], m_ctx[1])
        if not use_ssd:
            y_lat = short_conv_mixer(h_lat, sc_w_in[j], sc_conv_w[j], sc_w_out[j], grid_conv)
            if not last:
                y_ctx = short_conv_mixer(h_ctx, sc_w_in[j], sc_conv_w[j], sc_w_out[j], dwconv_seq)
        else:
            h0 = jnp.zeros((bsz, SSD_GROUPS, HEADS_PER_GROUP, HEAD_DIM, D_STATE), jnp.float32)
            yc, zc, h_f, h_b = ssd_stream(h_ctx, ssd_w_in[j], ssd_conv_w[j], ssd_conv_b[j], ssd_dt_bias[j],
                                          ssd_a_log[j], ssd_d[j], dwconv_seq, h0, h0)
            yl, zl, _, _ = ssd_stream(h_lat, ssd_w_in[j], ssd_conv_w[j], ssd_conv_b[j], ssd_dt_bias[j],
                                      ssd_a_log[j], ssd_d[j], grid_conv, h_f, h_b)
            y_lat = ssd_out(yl, zl, ssd_norm_w[j], ssd_w_out[j])
            if not last:
                y_ctx = ssd_out(yc, zc, ssd_norm_w[j], ssd_w_out[j])
        lat = lat + m_lat[2] * y_lat
        moe_w = (rg_w[i], rg_b[i], re_w[i], re_b[i], moe_w_gate[i], moe_w_up[i], moe_w_down[i])
        f_lat = modulate(lat, norm_ffn_w[i], m_lat[3], m_lat[4])
        if last:
            lat = lat + m_lat[5] * hier_moe(f_lat, *moe_w)
        else:
            cx = cx + m_ctx[2] * y_ctx
            f_ctx = modulate(cx, norm_ffn_w[i], m_ctx[3], m_ctx[4])
            n_ctx = cx.shape[1]
            f_out = hier_moe(jnp.concatenate([f_ctx, f_lat], axis=1), *moe_w)
            cx = cx + m_ctx[5] * f_out[:, :n_ctx]
            lat = lat + m_lat[5] * f_out[:, n_ctx:]
    return rms_norm(lat, final_norm_w)
```

```python
import functools

import jax
import jax.numpy as jnp
from jax import lax
from jax.experimental import pallas as pl
from jax.experimental.pallas import tpu as pltpu

F32 = jnp.float32
BF16 = jnp.bfloat16
I32 = jnp.int32

EPS = 1e-6
GRID_W = 64
HEAD_DIM = 64
D_STATE = 128
SSD_GROUPS = 8
CHUNK = 128
TOP_K = 2

LANES = 128
SUBLANES = 8
VMEM_LIMIT = 58 * 1024 * 1024
TOKEN_TILE = 512
COL_TILE = 512
GROUP_TILE = 256
ADA_COL_TILE = 1024


def _params(*sem):
    return pltpu.CompilerParams(dimension_semantics=sem, vmem_limit_bytes=VMEM_LIMIT)


def _dot(a, b):
    return jnp.dot(a, b, preferred_element_type=F32)


def _silu(x):
    return x * jax.nn.sigmoid(x)


def _mod_norm(x, nw, shift, scale):
    ms = jnp.mean(x * x, axis=-1, keepdims=True)
    y = x * lax.rsqrt(ms + EPS)
    return (y * nw) * (1.0 + scale) + shift


def _row_pos(shape, row_len):
    return lax.broadcasted_iota(I32, shape, 0) & (row_len - 1)


def _dwconv_rows(u, cw_ref, row_len):
    tm = u.shape[0]
    width = cw_ref.shape[0]
    half = width // 2
    pos = _row_pos(u.shape, row_len)
    out = cw_ref[half:half + 1, :] * u
    for k in range(width):
        off = k - half
        if off == 0:
            continue
        shifted = pltpu.roll(u, (-off) % tm, axis=0)
        valid = (pos + off >= 0) & (pos + off < row_len)
        out = out + cw_ref[k:k + 1, :] * jnp.where(valid, shifted, 0.0)
    return out


def _ada_kernel(c_ref, w_ref, b_ref, o_ref):
    s = _silu(c_ref[...])
    o_ref[...] = jnp.dot(s, w_ref[...], preferred_element_type=F32,
                         precision=lax.Precision.HIGHEST) + b_ref[...]


def _ada_mods(cvec, ada_w, ada_b):
    depth, d, d6 = ada_w.shape
    tn = min(ADA_COL_TILE, d6)
    out = pl.pallas_call(
        _ada_kernel,
        grid=(depth, d6 // tn),
        in_specs=[pl.BlockSpec((SUBLANES, d), lambda l, j: (0, 0)),
                  pl.BlockSpec((None, d, tn), lambda l, j: (l, 0, j)),
                  pl.BlockSpec((None, 1, tn), lambda l, j: (l, 0, j))],
        out_specs=pl.BlockSpec((None, SUBLANES, tn), lambda l, j: (l, 0, j)),
        out_shape=jax.ShapeDtypeStruct((depth, SUBLANES, d6), F32),
        compiler_params=_params("arbitrary", "arbitrary"),
        name="ada_mods",
    )(cvec, ada_w, ada_b.reshape(depth, 1, d6))
    return out.reshape(depth, SUBLANES, 6, d)


def _sc_kernel(xl_ref, xc_ref, m_ref, nw_ref, wb_ref, wc_ref, wv_ref, cw_ref, wo_ref,
               o_ref, h_ref, acc_ref, *, n_lat_tiles, ctx_row):
    i = pl.program_id(0)
    j = pl.program_id(1)
    is_lat = i < n_lat_tiles

    @pl.when(j == 0)
    def _():
        x = jnp.where(is_lat, xl_ref[...], xc_ref[...])
        h_ref[...] = _mod_norm(x, nw_ref[...], m_ref[0:1, :], m_ref[1:2, :]).astype(BF16)
        acc_ref[...] = jnp.zeros_like(acc_ref)

    h = h_ref[...]
    gate_b = _dot(h, wb_ref[...])
    u = _dot(h, wc_ref[...]) * _dot(h, wv_ref[...])
    row_len = jnp.where(is_lat, GRID_W, ctx_row)
    g = (gate_b * _dwconv_rows(u, cw_ref, row_len)).astype(BF16)
    acc_ref[...] += _dot(g, wo_ref[...])

    @pl.when(j == pl.num_programs(1) - 1)
    def _():
        x = jnp.where(is_lat, xl_ref[...], xc_ref[...])
        o_ref[...] = x + m_ref[2:3, :] * acc_ref[...]


def _short_conv_layer(x_lat, x_ctx, mods, layer, nw, w_in, conv_w, w_out, *, tiles_per_batch, ctx_row):
    t_lat, d = x_lat.shape
    t_ctx = x_ctx.shape[0]
    tm, tc = TOKEN_TILE, min(COL_TILE, d)
    n_lat, n_ctx, nj = t_lat // tm, t_ctx // tm, d // tc
    n_batch_rows = mods.shape[1]

    def mod_row(i):
        return jnp.minimum(i // tiles_per_batch, n_batch_rows - 1)

    kern = functools.partial(_sc_kernel, n_lat_tiles=n_lat, ctx_row=ctx_row)
    return pl.pallas_call(
        kern,
        grid=(n_lat + n_ctx, nj),
        in_specs=[
            pl.BlockSpec((tm, d), lambda i, j: (jnp.minimum(i, n_lat - 1), 0)),
            pl.BlockSpec((tm, d), lambda i, j: (jnp.maximum(i - n_lat, 0), 0)),
            pl.BlockSpec((None, None, 6, d), lambda i, j: (layer, mod_row(i), 0, 0)),
            pl.BlockSpec((1, d), lambda i, j: (0, 0)),
            pl.BlockSpec((d, tc), lambda i, j: (0, j)),
            pl.BlockSpec((d, tc), lambda i, j: (0, nj + j)),
            pl.BlockSpec((d, tc), lambda i, j: (0, 2 * nj + j)),
            pl.BlockSpec((conv_w.shape[0], tc), lambda i, j: (0, j)),
            pl.BlockSpec((tc, d), lambda i, j: (j, 0)),
        ],
        out_specs=pl.BlockSpec((tm, d), lambda i, j: (i, 0)),
        out_shape=jax.ShapeDtypeStruct((t_lat + t_ctx, d), F32),
        scratch_shapes=[pltpu.VMEM((tm, d), BF16), pltpu.VMEM((tm, d), F32)],
        compiler_params=_params("arbitrary", "arbitrary"),
        name="short_conv_mixer",
    )(x_lat, x_ctx, mods, nw, w_in, w_in, w_in, conv_w, w_out)


def _route_kernel(x_ref, m_ref, nw_ref, wr_ref, br_ref, f_ref, meta_ref, wts_ref, cnt_ref, run_ref,
                  *, n_groups, per_group):
    i = pl.program_id(0)

    @pl.when(i == 0)
    def _():
        run_ref[...] = jnp.zeros_like(run_ref)

    f = _mod_norm(x_ref[...], nw_ref[...], m_ref[3:4, :], m_ref[4:5, :])
    f_ref[...] = f
    logits = _dot(f.astype(BF16), wr_ref[...]) + br_ref[...]
    tm = logits.shape[0]
    lane = lax.broadcasted_iota(I32, logits.shape, 1).astype(F32)
    neg = -jnp.inf

    g_mask = lane < n_groups
    gl = jnp.where(g_mask, logits, neg)
    g_max = jnp.max(gl, axis=1, keepdims=True)
    g_sel = jnp.min(jnp.where(gl == g_max, lane, LANES), axis=1, keepdims=True)
    p_group = 1.0 / jnp.sum(jnp.where(g_mask, jnp.exp(gl - g_max), 0.0), axis=1, keepdims=True)

    e_lo = n_groups + g_sel * per_group
    in_group = (lane >= e_lo) & (lane < e_lo + per_group)
    el = jnp.where(in_group, logits, neg)
    e_max = jnp.max(el, axis=1, keepdims=True)
    ex = jnp.where(in_group, jnp.exp(el - e_max), 0.0)
    p = jnp.where(in_group, ex / jnp.sum(ex, axis=1, keepdims=True), -1.0)
    p1 = jnp.max(p, axis=1, keepdims=True)
    l1 = jnp.min(jnp.where(p == p1, lane, LANES), axis=1, keepdims=True)
    p_rest = jnp.where(lane == l1, -1.0, p)
    p2 = jnp.max(p_rest, axis=1, keepdims=True)
    l2 = jnp.min(jnp.where(p_rest == p2, lane, LANES), axis=1, keepdims=True)
    denom = p1 + p2
    w1 = p1 / denom * p_group
    w2 = p2 / denom * p_group

    hit1 = lane == l1
    hit2 = lane == l2
    onehot = jnp.where(hit1 | hit2, 1.0, 0.0)
    earlier = lax.broadcasted_iota(I32, (tm, tm), 0) > lax.broadcasted_iota(I32, (tm, tm), 1)
    before = _dot(jnp.where(earlier, 1.0, 0.0).astype(BF16), onehot.astype(BF16)) + run_ref[...]
    r1 = jnp.sum(jnp.where(hit1, before, 0.0), axis=1, keepdims=True)
    r2 = jnp.sum(jnp.where(hit2, before, 0.0), axis=1, keepdims=True)
    run_ref[...] += jnp.sum(onehot, axis=0, keepdims=True)
    cnt_ref[...] = run_ref[...].astype(I32)

    meta = jnp.where(lane == 0, l1 - n_groups,
                     jnp.where(lane == 1, l2 - n_groups, jnp.where(lane == 2, r1, r2)))
    meta_ref[...] = meta.astype(I32)
    wts_ref[...] = jnp.where(lane == 0, w1, w2)


def _moe_route(tokens, n_tok, mods, layer, nw, wr, br, *, tiles_per_batch, n_groups, per_group):
    d = tokens.shape[1]
    tm = TOKEN_TILE
    n_batch_rows = mods.shape[1]

    def mod_row(i):
        return jnp.minimum(i // tiles_per_batch, n_batch_rows - 1)

    kern = functools.partial(_route_kernel, n_groups=n_groups, per_group=per_group)
    return pl.pallas_call(
        kern,
        grid=(n_tok // tm,),
        in_specs=[
            pl.BlockSpec((tm, d), lambda i: (i, 0)),
            pl.BlockSpec((None, None, 6, d), lambda i: (layer, mod_row(i), 0, 0)),
            pl.BlockSpec((1, d), lambda i: (0, 0)),
            pl.BlockSpec((d, LANES), lambda i: (0, 0)),
            pl.BlockSpec((1, LANES), lambda i: (0, 0)),
        ],
        out_specs=[
            pl.BlockSpec((tm, d), lambda i: (i, 0)),
            pl.BlockSpec((tm, LANES), lambda i: (i, 0)),
            pl.BlockSpec((tm, LANES), lambda i: (i, 0)),
            pl.BlockSpec((1, LANES), lambda i: (0, 0)),
        ],
        out_shape=[
            jax.ShapeDtypeStruct((n_tok, d), F32),
            jax.ShapeDtypeStruct((n_tok, LANES), I32),
            jax.ShapeDtypeStruct((n_tok, LANES), F32),
            jax.ShapeDtypeStruct((1, LANES), I32),
        ],
        scratch_shapes=[pltpu.VMEM((1, LANES), F32)],
        compiler_params=_params("arbitrary"),
        name="moe_route",
    )(tokens, mods, nw, wr, br)


def _route_plan(meta, counts, *, n_tok, n_exp, n_groups, tg):
    ids = meta[:, 0:TOP_K]
    rank = meta[:, TOP_K:2 * TOP_K]
    cnt = counts[0, n_groups:n_groups + n_exp]
    padded = (cnt + tg - 1) // tg * tg
    ends = jnp.cumsum(padded)
    pos = (ends - padded)[ids] + rank
    n_tiles = -(-(TOP_K * n_tok) // tg) + n_exp
    tok = lax.broadcasted_iota(I32, (n_tok, TOP_K), 0)
    slot_row = lax.broadcasted_iota(I32, (n_tok, TOP_K), 1) * n_tok + tok
    src = jnp.zeros((n_tiles * tg,), I32).at[pos.reshape(-1)].set(tok.reshape(-1))
    dst = jnp.zeros((n_tiles * tg,), I32).at[pos.reshape(-1)].set(slot_row.reshape(-1))
    idx = jnp.stack([src.reshape(n_tiles, tg), dst.reshape(n_tiles, tg)], axis=1)
    tile_start = jnp.arange(n_tiles, dtype=I32) * tg
    tile_exp = jnp.minimum(jnp.searchsorted(ends, tile_start, side="right"), n_exp - 1).astype(I32)
    seg_end = (ends - padded + cnt)[tile_exp]
    n_valid = jnp.clip(seg_end - tile_start, 0, tg).astype(I32)
    n_used = (ends[-1] // tg).astype(I32).reshape(1)
    return idx, tile_exp, n_valid, n_used


def _gmm_kernel(te_ref, nv_ref, nu_ref, idx_ref, idxn_ref, f_hbm, wg_ref, wu_ref, wd_ref, y_hbm,
                xbuf, ybuf, gsem, ssem):
    j = pl.program_id(0)
    last = pl.num_programs(0) - 1
    n_used = nu_ref[0]
    slot = j % 2
    tg = xbuf.shape[1]

    def gather_copy(tok, r, s):
        return pltpu.make_async_copy(f_hbm.at[pl.ds(tok, 1), :], xbuf.at[s, pl.ds(r, 1), :], gsem.at[s])

    def scatter_copy(row, r, s):
        return pltpu.make_async_copy(ybuf.at[s, pl.ds(r, 1), :], y_hbm.at[pl.ds(row, 1), :], ssem.at[s])

    def start_gather(ref, s):
        def body(r, c):
            gather_copy(ref[0, 0, r], r, s).start()
            return c
        lax.fori_loop(0, tg, body, 0)

    def wait_gather(s):
        pltpu.make_async_copy(f_hbm.at[pl.ds(0, tg), :], xbuf.at[s], gsem.at[s]).wait()

    def start_scatter(ref, s, n_rows):
        def body(r, c):
            scatter_copy(ref[0, 1, r], r, s).start()
            return c
        lax.fori_loop(0, n_rows, body, 0)

    def wait_scatter(s, n_rows):
        n_al = pl.multiple_of(n_rows // SUBLANES * SUBLANES, SUBLANES)

        @pl.when(n_al > 0)
        def _():
            pltpu.make_async_copy(ybuf.at[s, pl.ds(0, n_al), :], y_hbm.at[pl.ds(0, n_al), :], ssem.at[s]).wait()

        def body(r, c):
            scatter_copy(0, 0, s).wait()
            return c
        lax.fori_loop(0, n_rows - n_al, body, 0)

    @pl.when(j == 0)
    def _():
        start_gather(idx_ref, 0)

    @pl.when(j < n_used)
    def _():
        wait_gather(slot)

        @pl.when(j + 1 < n_used)
        def _():
            start_gather(idxn_ref, 1 - slot)

        x = xbuf[slot].astype(BF16)
        hid = (_silu(_dot(x, wg_ref[...])) * _dot(x, wu_ref[...])).astype(BF16)
        ybuf[slot] = _dot(hid, wd_ref[...])
        start_scatter(idx_ref, slot, nv_ref[j])

        @pl.when(j >= 1)
        def _():
            wait_scatter(1 - slot, nv_ref[j - 1])

        @pl.when(j == last)
        def _():
            wait_scatter(slot, nv_ref[j])

    @pl.when((j == n_used) & (j >= 1))
    def _():
        wait_scatter(1 - slot, nv_ref[j - 1])


def _moe_experts(f, idx, tile_exp, n_valid, n_used, wg, wu, wd, *, n_tok):
    d = f.shape[1]
    n_tiles, _, tg = idx.shape
    de = wg.shape[2]
    out_rows = TOP_K * n_tok
    grid_spec = pltpu.PrefetchScalarGridSpec(
        num_scalar_prefetch=3,
        grid=(n_tiles,),
        in_specs=[
            pl.BlockSpec((1, 2, tg), lambda j, te, nv, nu: (j, 0, 0), memory_space=pltpu.SMEM),
            pl.BlockSpec((1, 2, tg), lambda j, te, nv, nu: (jnp.minimum(j + 1, n_tiles - 1), 0, 0),
                         memory_space=pltpu.SMEM),
            pl.BlockSpec(memory_space=pl.ANY),
            pl.BlockSpec((None, d, de), lambda j, te, nv, nu: (te[j], 0, 0)),
            pl.BlockSpec((None, d, de), lambda j, te, nv, nu: (te[j], 0, 0)),
            pl.BlockSpec((None, de, d), lambda j, te, nv, nu: (te[j], 0, 0)),
        ],
        out_specs=pl.BlockSpec(memory_space=pl.ANY),
        scratch_shapes=[
            pltpu.VMEM((2, tg, d), F32),
            pltpu.VMEM((2, tg, d), F32),
            pltpu.SemaphoreType.DMA((2,)),
            pltpu.SemaphoreType.DMA((2,)),
        ],
    )
    y = pl.pallas_call(
        _gmm_kernel,
        grid_spec=grid_spec,
        out_shape=jax.ShapeDtypeStruct((out_rows, d), F32),
        compiler_params=_params("arbitrary"),
        name="moe_experts",
    )(tile_exp, n_valid, n_used, idx, idx, f, wg, wu, wd)
    return y.reshape(TOP_K, n_tok, d)


def _moe_combine(y0_ref, y1_ref, wts_ref):
    wts = wts_ref[...]
    return wts[:, 0:1] * y0_ref[...] + wts[:, 1:2] * y1_ref[...]


def _inproj_kernel(x_ref, y0_ref, y1_ref, wts_ref, mp_ref, m_ref, nw_ref, w_ref, cw_ref, cb_ref, dtb_ref,
                   lat_ref, z_ref, xbc_ref, dt_ref, h_ref, *, nz, nx, n_lat_tiles, ctx_row):
    i = pl.program_id(0)
    j = pl.program_id(1)

    @pl.when(j == 0)
    def _():
        lat = x_ref[...] + mp_ref[5:6, :] * _moe_combine(y0_ref, y1_ref, wts_ref)
        lat_ref[...] = lat
        h_ref[...] = _mod_norm(lat, nw_ref[...], m_ref[0:1, :], m_ref[1:2, :]).astype(BF16)

    r = _dot(h_ref[...], w_ref[...])

    @pl.when(j < nz)
    def _():
        z_ref[...] = r.astype(BF16)

    @pl.when((j >= nz) & (j < nz + nx))
    def _():
        row_len = jnp.where(i < n_lat_tiles, GRID_W, ctx_row)
        xbc_ref[...] = _silu(_dwconv_rows(r, cw_ref, row_len) + cb_ref[...]).astype(BF16)

    @pl.when(j == nz + nx)
    def _():
        v = r[:, 0:LANES] + dtb_ref[...]
        dt_ref[...] = jnp.maximum(v, 0.0) + jnp.log1p(jnp.exp(-jnp.abs(v)))


def _ssd_in_proj(tokens, y2, wts, mods, layer, nw, w_cat, conv_w, conv_b, dt_bias,
                 *, tiles_per_batch, n_lat_tiles, ctx_row, d_inner, conv_dim):
    t, d = tokens.shape
    tm, tc = TOKEN_TILE, COL_TILE
    nz, nx = d_inner // tc, conv_dim // tc
    nj = nz + nx + 1
    n_batch_rows = mods.shape[1]

    def mod_row(i):
        return jnp.minimum(i // tiles_per_batch, n_batch_rows - 1)

    def xcol(j):
        return jnp.clip(j - nz, 0, nx - 1)

    kern = functools.partial(_inproj_kernel, nz=nz, nx=nx, n_lat_tiles=n_lat_tiles, ctx_row=ctx_row)
    return pl.pallas_call(
        kern,
        grid=(t // tm, nj),
        in_specs=[
            pl.BlockSpec((tm, d), lambda i, j: (i, 0)),
            pl.BlockSpec((None, tm, d), lambda i, j: (0, i, 0)),
            pl.BlockSpec((None, tm, d), lambda i, j: (1, i, 0)),
            pl.BlockSpec((tm, LANES), lambda i, j: (i, 0)),
            pl.BlockSpec((None, None, 6, d), lambda i, j: (layer - 1, mod_row(i), 0, 0)),
            pl.BlockSpec((None, None, 6, d), lambda i, j: (layer, mod_row(i), 0, 0)),
            pl.BlockSpec((1, d), lambda i, j: (0, 0)),
            pl.BlockSpec((d, tc), lambda i, j: (0, j)),
            pl.BlockSpec((conv_w.shape[0], tc), lambda i, j: (0, xcol(j))),
            pl.BlockSpec((1, tc), lambda i, j: (0, xcol(j))),
            pl.BlockSpec((1, LANES), lambda i, j: (0, 0)),
        ],
        out_specs=[
            pl.BlockSpec((tm, d), lambda i, j: (i, 0)),
            pl.BlockSpec((tm, tc), lambda i, j: (i, jnp.minimum(j, nz - 1))),
            pl.BlockSpec((tm, tc), lambda i, j: (i, xcol(j))),
            pl.BlockSpec((tm, LANES), lambda i, j: (i, 0)),
        ],
        out_shape=[
            jax.ShapeDtypeStruct((t, d), F32),
            jax.ShapeDtypeStruct((t, d_inner), BF16),
            jax.ShapeDtypeStruct((t, conv_dim), BF16),
            jax.ShapeDtypeStruct((t, LANES), F32),
        ],
        scratch_shapes=[pltpu.VMEM((tm, d), BF16)],
        compiler_params=_params("arbitrary", "arbitrary"),
        name="ssd_in_proj",
    )(tokens, y2, y2, wts, mods, mods, nw, w_cat, conv_w, conv_b, dt_bias)


def _split_dot(a, e):
    hi = a.astype(BF16)
    lo = (a - hi.astype(F32)).astype(BF16)
    return _dot(hi, e) + _dot(lo, e)


def _ssd_kernel(x_ref, b_ref, c_ref, dt_ref, a_ref, y_ref, st_ref, *, n_heads):
    d = pl.program_id(1)
    k = pl.program_id(2)
    q = x_ref.shape[0]
    n_groups = st_ref.shape[0]
    gw = st_ref.shape[2]
    hpg = gw // HEAD_DIM
    fwd = d == 0

    @pl.when(k == 0)
    def _():
        st_ref[...] = jnp.zeros_like(st_ref)

    t_i = lax.broadcasted_iota(I32, (q, q), 0)
    s_i = lax.broadcasted_iota(I32, (q, q), 1)
    incl = jnp.where(fwd, t_i - s_i, s_i - t_i) >= 0

    dt_all = dt_ref[...]
    dta_all = dt_all * a_ref[...]
    cum_all = jnp.dot(jnp.where(incl, 1.0, 0.0), dta_all, preferred_element_type=F32,
                      precision=lax.Precision.HIGHEST)
    tot_all = jnp.sum(dta_all, axis=0, keepdims=True)
    cum_t_all = cum_all.T

    def pick(a):
        return jnp.where(fwd, a[:, 0:n_heads], a[:, n_heads:2 * n_heads])

    dt = pick(dt_all)
    cum = pick(cum_all)
    tot = pick(tot_all)
    cum_t = jnp.where(fwd, cum_t_all[0:n_heads, :], cum_t_all[n_heads:2 * n_heads, :])
    dec_in = jnp.exp(cum)
    dec_out = jnp.exp(tot - cum)
    dec_all = jnp.broadcast_to(jnp.exp(tot), (SUBLANES, n_heads))

    lane_h = lax.broadcasted_iota(I32, (n_heads, gw), 1) // HEAD_DIM
    head_i = lax.broadcasted_iota(I32, (n_heads, gw), 0)
    low_half = lax.broadcasted_iota(I32, (q, 2 * HEAD_DIM), 1) < HEAD_DIM

    for g in range(n_groups):
        expand = jnp.where(head_i == g * hpg + lane_h, 1.0, 0.0).astype(BF16)
        bg = b_ref[:, g * D_STATE:(g + 1) * D_STATE]
        cg = c_ref[:, g * D_STATE:(g + 1) * D_STATE]
        xdt = x_ref[:, g * gw:(g + 1) * gw].astype(F32) * _split_dot(dt, expand)
        xdt_b = xdt.astype(BF16)
        cb = lax.dot_general(cg, bg, (((1,), (1,)), ((), ())), preferred_element_type=F32)
        cb = jnp.where(incl, cb, 0.0)

        parts = []
        for pair in range(hpg // 2):
            xp = xdt_b[:, pair * 2 * HEAD_DIM:(pair + 1) * 2 * HEAD_DIM]
            ys = []
            for hh in range(2):
                h = g * hpg + pair * 2 + hh
                diff = cum[:, h:h + 1] - cum_t[h:h + 1, :]
                w = (cb * jnp.exp(jnp.minimum(diff, 0.0))).astype(BF16)
                ys.append(_dot(w, xp))
            parts.append(jnp.where(low_half, ys[0], ys[1]))
        y_diag = parts[0] if len(parts) == 1 else jnp.concatenate(parts, axis=1)

        st = st_ref[g]
        y_off = _dot(cg, st.astype(BF16)) * _split_dot(dec_in, expand)
        y_ref[:, g * gw:(g + 1) * gw] = (y_diag + y_off).astype(BF16)

        xe = (xdt * _split_dot(dec_out, expand)).astype(BF16)
        upd = lax.dot_general(bg, xe, (((0,), (0,)), ((), ())), preferred_element_type=F32)
        st_ref[g] = st * _split_dot(dec_all, expand)[0:1, :] + upd


def _ssd_scan(xbc, dt, a_neg, *, n_batch, lat_chunks, ctx_chunks, d_inner, n_heads):
    t = xbc.shape[0]
    q = CHUNK
    gn = SSD_GROUPS * D_STATE
    gw = d_inner // SSD_GROUPS
    assert gw % (2 * HEAD_DIM) == 0 and d_inner % gn == 0
    n_steps = ctx_chunks + lat_chunks
    ctx_base = n_batch * lat_chunks

    def chunk(b, d, k):
        kc = jnp.where(d == 0, k, ctx_chunks - 1 - k)
        kl = jnp.where(d == 0, k - ctx_chunks, n_steps - 1 - k)
        return jnp.where(k < ctx_chunks, ctx_base + b * ctx_chunks + kc, b * lat_chunks + kl)

    kern = functools.partial(_ssd_kernel, n_heads=n_heads)
    return pl.pallas_call(
        kern,
        grid=(n_batch, 2, n_steps),
        in_specs=[
            pl.BlockSpec((q, d_inner), lambda b, d, k: (chunk(b, d, k), 0)),
            pl.BlockSpec((q, gn), lambda b, d, k: (chunk(b, d, k), d_inner // gn)),
            pl.BlockSpec((q, gn), lambda b, d, k: (chunk(b, d, k), d_inner // gn + 1)),
            pl.BlockSpec((q, LANES), lambda b, d, k: (chunk(b, d, k), 0)),
            pl.BlockSpec((1, LANES), lambda b, d, k: (0, 0)),
        ],
        out_specs=pl.BlockSpec((None, q, d_inner), lambda b, d, k: (d, chunk(b, d, k), 0)),
        out_shape=jax.ShapeDtypeStruct((2, t, d_inner), BF16),
        scratch_shapes=[pltpu.VMEM((SSD_GROUPS, D_STATE, gw), F32)],
        compiler_params=_params("arbitrary", "arbitrary", "arbitrary"),
        name="ssd_scan",
    )(xbc, xbc, xbc, dt, a_neg)


def _outproj_kernel(yf_ref, yb_ref, xs_ref, z_ref, dsk_ref, gnw_ref, lat_ref, m_ref, w_ref, o_ref, gn_ref):
    j = pl.program_id(1)

    @pl.when(j == 0)
    def _():
        gw = gn_ref.shape[1] // SSD_GROUPS
        for g in range(SSD_GROUPS):
            sl = slice(g * gw, (g + 1) * gw)
            y = (yf_ref[:, sl].astype(F32) + yb_ref[:, sl].astype(F32)
                 + dsk_ref[:, sl] * xs_ref[:, sl].astype(F32))
            gated = y * _silu(z_ref[:, sl].astype(F32))
            ms = jnp.mean(gated * gated, axis=-1, keepdims=True)
            gn_ref[:, sl] = (gated * lax.rsqrt(ms + EPS) * gnw_ref[:, sl]).astype(BF16)

    o_ref[...] = lat_ref[...] + m_ref[2:3, :] * _dot(gn_ref[...], w_ref[...])


def _ssd_out_proj(y, xbc, z, d_skip, gnw, tokens, mods, layer, w_out, *, n_tok, tiles_per_batch):
    d_inner, d = w_out.shape
    tm, tc = TOKEN_TILE, min(COL_TILE, d)
    n_batch_rows = mods.shape[1]

    def mod_row(i):
        return jnp.minimum(i // tiles_per_batch, n_batch_rows - 1)

    return pl.pallas_call(
        _outproj_kernel,
        grid=(n_tok // tm, d // tc),
        in_specs=[
            pl.BlockSpec((None, tm, d_inner), lambda i, j: (0, i, 0)),
            pl.BlockSpec((None, tm, d_inner), lambda i, j: (1, i, 0)),
            pl.BlockSpec((tm, d_inner), lambda i, j: (i, 0)),
            pl.BlockSpec((tm, d_inner), lambda i, j: (i, 0)),
            pl.BlockSpec((1, d_inner), lambda i, j: (0, 0)),
            pl.BlockSpec((1, d_inner), lambda i, j: (0, 0)),
            pl.BlockSpec((tm, tc), lambda i, j: (i, j)),
            pl.BlockSpec((None, None, 6, tc), lambda i, j: (layer, mod_row(i), 0, j)),
            pl.BlockSpec((d_inner, tc), lambda i, j: (0, j)),
        ],
        out_specs=pl.BlockSpec((tm, tc), lambda i, j: (i, j)),
        out_shape=jax.ShapeDtypeStruct((n_tok, d), F32),
        scratch_shapes=[pltpu.VMEM((tm, d_inner), BF16)],
        compiler_params=_params("arbitrary", "arbitrary"),
        name="ssd_out_proj",
    )(y, y, xbc, z, d_skip, gnw, tokens, mods, w_out)


def _final_kernel(x_ref, y0_ref, y1_ref, wts_ref, m_ref, nw_ref, o_ref):
    lat = x_ref[...] + m_ref[5:6, :] * _moe_combine(y0_ref, y1_ref, wts_ref)
    ms = jnp.mean(lat * lat, axis=-1, keepdims=True)
    o_ref[...] = lat * lax.rsqrt(ms + EPS) * nw_ref[...]


def _final_norm(tokens, y2, wts, mods, layer, nw, *, tiles_per_batch):
    t, d = tokens.shape
    tm = TOKEN_TILE
    n_batch_rows = mods.shape[1]

    def mod_row(i):
        return jnp.minimum(i // tiles_per_batch, n_batch_rows - 1)

    return pl.pallas_call(
        _final_kernel,
        grid=(t // tm,),
        in_specs=[
            pl.BlockSpec((tm, d), lambda i: (i, 0)),
            pl.BlockSpec((None, tm, d), lambda i: (0, i, 0)),
            pl.BlockSpec((None, tm, d), lambda i: (1, i, 0)),
            pl.BlockSpec((tm, LANES), lambda i: (i, 0)),
            pl.BlockSpec((None, None, 6, d), lambda i: (layer, mod_row(i), 0, 0)),
            pl.BlockSpec((1, d), lambda i: (0, 0)),
        ],
        out_specs=pl.BlockSpec((tm, d), lambda i: (i, 0)),
        out_shape=jax.ShapeDtypeStruct((t, d), F32),
        compiler_params=_params("arbitrary"),
        name="final_norm",
    )(tokens, y2, y2, wts, mods, nw)


def _router_weights(rg_w, rg_b, re_w, re_b):
    d, n_groups = rg_w.shape
    n_exp = re_w.shape[1]
    pad = LANES - n_groups - n_exp
    wr = jnp.concatenate([rg_w, re_w, jnp.zeros((d, pad), F32)], axis=1).astype(BF16)
    br = jnp.concatenate([rg_b, re_b, jnp.zeros((pad,), F32)]).reshape(1, LANES)
    return wr, br


def _moe(tokens, n_tok, mods, layer, nw, rg_w, rg_b, re_w, re_b, w_gate, w_up, w_down, *, tiles_per_batch):
    n_groups = rg_w.shape[1]
    n_exp = re_w.shape[1]
    wr, br = _router_weights(rg_w, rg_b, re_w, re_b)
    f, meta, wts, counts = _moe_route(tokens, n_tok, mods, layer, nw, wr, br, tiles_per_batch=tiles_per_batch,
                                      n_groups=n_groups, per_group=n_exp // n_groups)
    idx, tile_exp, n_valid, n_used = _route_plan(meta, counts, n_tok=n_tok, n_exp=n_exp, n_groups=n_groups,
                                                 tg=GROUP_TILE)
    y2 = _moe_experts(f, idx, tile_exp, n_valid, n_used, w_gate.astype(BF16), w_up.astype(BF16),
                      w_down.astype(BF16), n_tok=n_tok)
    return y2, wts


def kernel(x, c, ctx, c_ctx, ada_w, ada_b, norm_mix_w, norm_ffn_w, sc_w_in, sc_conv_w, sc_w_out, ssd_w_in,
           ssd_conv_w, ssd_conv_b, ssd_dt_bias, ssd_a_log, ssd_d, ssd_norm_w, ssd_w_out, rg_w, rg_b, re_w, re_b,
           moe_w_gate, moe_w_up, moe_w_down, final_norm_w):
    n_batch, seq, d = x.shape
    ctx_len = ctx.shape[1]
    depth = ada_w.shape[0]
    d_inner = ssd_w_out.shape[1]
    conv_dim = ssd_conv_w.shape[2]
    n_heads = ssd_d.shape[1]
    tm = TOKEN_TILE
    t_lat, t_ctx = n_batch * seq, n_batch * ctx_len
    assert depth == 2 and seq % tm == 0 and t_ctx % tm == 0 and tm % ctx_len == 0 and tm % GRID_W == 0
    assert ctx_len & (ctx_len - 1) == 0 and GRID_W & (GRID_W - 1) == 0
    assert seq % CHUNK == 0 and ctx_len % CHUNK == 0 and d_inner == n_heads * HEAD_DIM
    assert n_batch + 1 <= SUBLANES and 2 * n_heads <= LANES and d_inner % COL_TILE == 0 and conv_dim % COL_TILE == 0
    tiles_per_batch = seq // tm
    n_lat_tiles = t_lat // tm

    cvec = jnp.zeros((SUBLANES, d), F32).at[0:n_batch].set(c).at[n_batch].set(c_ctx)
    mods = _ada_mods(cvec, ada_w, ada_b)

    x_lat = x.reshape(t_lat, d)
    x_ctx = ctx.reshape(t_ctx, d)

    tokens = _short_conv_layer(x_lat, x_ctx, mods, 0, norm_mix_w[0:1], sc_w_in[0].astype(BF16), sc_conv_w[0],
                               sc_w_out[0].astype(BF16), tiles_per_batch=tiles_per_batch, ctx_row=ctx_len)
    t_all = t_lat + t_ctx
    y2, wts = _moe(tokens, t_all, mods, 0, norm_ffn_w[0:1], rg_w[0], rg_b[0], re_w[0], re_b[0],
                   moe_w_gate[0], moe_w_up[0], moe_w_down[0], tiles_per_batch=tiles_per_batch)

    w_in = ssd_w_in[0]
    dt_cols = 2 * n_heads
    w_cat = jnp.concatenate([w_in[:, :d_inner + conv_dim], w_in[:, d_inner + conv_dim:],
                             jnp.zeros((d, COL_TILE - dt_cols), F32)], axis=1).astype(BF16)
    dt_bias = jnp.zeros((1, LANES), F32).at[0, :dt_cols].set(ssd_dt_bias[0].reshape(-1))
    a_neg = jnp.zeros((1, LANES), F32).at[0, :dt_cols].set(-jnp.exp(ssd_a_log[0].reshape(-1)))
    tokens, z, xbc, dt = _ssd_in_proj(tokens, y2, wts, mods, 1, norm_mix_w[1:2], w_cat, ssd_conv_w[0],
                                      ssd_conv_b[0].reshape(1, conv_dim), dt_bias,
                                      tiles_per_batch=tiles_per_batch, n_lat_tiles=n_lat_tiles, ctx_row=ctx_len,
                                      d_inner=d_inner, conv_dim=conv_dim)
    y = _ssd_scan(xbc, dt, a_neg, n_batch=n_batch, lat_chunks=seq // CHUNK, ctx_chunks=ctx_len // CHUNK,
                  d_inner=d_inner, n_heads=n_heads)
    d_skip = jnp.repeat(ssd_d[0], HEAD_DIM).reshape(1, d_inner)
    lat = _ssd_out_proj(y, xbc, z, d_skip, ssd_norm_w[0].reshape(1, d_inner), tokens, mods, 1,
                        ssd_w_out[0].astype(BF16), n_tok=t_lat, tiles_per_batch=tiles_per_batch)

    y2, wts = _moe(lat, t_lat, mods, 1, norm_ffn_w[1:2], rg_w[1], rg_b[1], re_w[1], re_b[1],
                   moe_w_gate[1], moe_w_up[1], moe_w_down[1], tiles_per_batch=tiles_per_batch)
    out = _final_norm(lat, y2, wts, mods, 1, final_norm_w.reshape(1, d), tiles_per_batch=tiles_per_batch)
    return out.reshape(n_batch, seq, d)
```

```python
import functools

import jax
import jax.numpy as jnp
from jax import lax
from jax.experimental import pallas as pl
from jax.experimental.pallas import tpu as pltpu

F32 = jnp.float32
BF16 = jnp.bfloat16
I32 = jnp.int32
U32 = jnp.uint32

EPS = 1e-6
GRID_W = 64
HEAD_DIM = 64
D_STATE = 128
SSD_GROUPS = 8
CHUNK = 128
TOP_K = 2

LANES = 128
SUBLANES = 8
VMEM_LIMIT = 58 * 1024 * 1024
TOKEN_TILE = 512
COL_TILE = 512
OUT_TOKEN_TILE = 256
GROUP_TILE = 256
PERM_TILE = 512
ADA_COL_TILE = 1024
HI_MASK = 0xFFFF0000


def _params(*sem):
    return pltpu.CompilerParams(dimension_semantics=sem, vmem_limit_bytes=VMEM_LIMIT)


def _dot(a, b):
    return jnp.dot(a, b, preferred_element_type=F32)


def _silu(x):
    return x * jax.nn.sigmoid(x)


def _mod_norm(x, nw, shift, scale):
    ms = jnp.mean(x * x, axis=-1, keepdims=True)
    y = x * lax.rsqrt(ms + EPS)
    return (y * nw) * (1.0 + scale) + shift


def _pack_pair(lo, hi):
    lo_bits = lax.bitcast_convert_type(lo.astype(BF16).astype(F32), U32)
    hi_bits = lax.bitcast_convert_type(hi.astype(BF16).astype(F32), U32)
    return (lo_bits >> 16) | (hi_bits & jnp.uint32(HI_MASK))


def _unpack_pair(u):
    lo = lax.bitcast_convert_type(u << 16, F32)
    hi = lax.bitcast_convert_type(u & jnp.uint32(HI_MASK), F32)
    return lo, hi


def _pack_row(x):
    half = x.shape[1] // 2
    return _pack_pair(x[:, :half], x[:, half:])


def _unpack_row(u):
    lo, hi = _unpack_pair(u)
    return jnp.concatenate([lo, hi], axis=1)


def _conv_masks(mask_ref, row_len):
    n_side, tm, _ = mask_ref.shape
    half = n_side // 2
    pos = lax.broadcasted_iota(I32, (tm, LANES), 0) & (row_len - 1)
    for m in range(n_side):
        off = m - half if m < half else m - half + 1
        valid = (pos + off >= 0) & (pos + off < row_len)
        mask_ref[m] = jnp.where(valid, 1.0, 0.0)


def _dwconv_rows(u, cw_ref, mask_ref, col0):
    tm, width_c = u.shape
    width = cw_ref.shape[0]
    half = width // 2
    outs = []
    for c in range(width_c // LANES):
        sl = slice(c * LANES, (c + 1) * LANES)
        wsl = slice(col0 + c * LANES, col0 + (c + 1) * LANES)
        uc = u[:, sl]
        out = cw_ref[half:half + 1, wsl] * uc
        for k in range(width):
            off = k - half
            if off == 0:
                continue
            m = k if k < half else k - 1
            shifted = pltpu.roll(uc, (-off) % tm, axis=0)
            out = out + cw_ref[k:k + 1, wsl] * (shifted * mask_ref[m])
        outs.append(out)
    return outs[0] if len(outs) == 1 else jnp.concatenate(outs, axis=1)


def _ada_kernel(c_ref, w_ref, b_ref, o_ref):
    s = _silu(c_ref[...])
    o_ref[...] = jnp.dot(s, w_ref[...], preferred_element_type=F32,
                         precision=lax.Precision.HIGHEST) + b_ref[...]


def _ada_mods(cvec, ada_w, ada_b):
    depth, d, d6 = ada_w.shape
    tn = min(ADA_COL_TILE, d6)
    out = pl.pallas_call(
        _ada_kernel,
        grid=(depth, d6 // tn),
        in_specs=[pl.BlockSpec((SUBLANES, d), lambda l, j: (0, 0)),
                  pl.BlockSpec((None, d, tn), lambda l, j: (l, 0, j)),
                  pl.BlockSpec((None, 1, tn), lambda l, j: (l, 0, j))],
        out_specs=pl.BlockSpec((None, SUBLANES, tn), lambda l, j: (l, 0, j)),
        out_shape=jax.ShapeDtypeStruct((depth, SUBLANES, d6), F32),
        compiler_params=_params("arbitrary", "arbitrary"),
        name="ada_mods",
    )(cvec, ada_w, ada_b.reshape(depth, 1, d6))
    return out.reshape(depth, SUBLANES, 6, d)


def _mod_row_fn(mods, tokens_per_batch, tile):
    n_rows = mods.shape[1]
    per_batch = tokens_per_batch // tile

    def mod_row(i):
        return jnp.minimum(i // per_batch, n_rows - 1)
    return mod_row


def _sc_kernel(xl_ref, xc_ref, m_ref, nw_ref, wb_ref, wc_ref, wv_ref, cw_ref, wo_ref,
               o_ref, h_ref, acc_ref, mask_ref, *, n_lat_tiles, ctx_row):
    i = pl.program_id(0)
    j = pl.program_id(1)
    is_lat = i < n_lat_tiles

    @pl.when(j == 0)
    def _():
        x = jnp.where(is_lat, xl_ref[...], xc_ref[...])
        h_ref[...] = _mod_norm(x, nw_ref[...], m_ref[0:1, :], m_ref[1:2, :]).astype(BF16)
        acc_ref[...] = jnp.zeros_like(acc_ref)
        _conv_masks(mask_ref, jnp.where(is_lat, GRID_W, ctx_row))

    h = h_ref[...]
    gate_b = _dot(h, wb_ref[...])
    u = _dot(h, wc_ref[...]) * _dot(h, wv_ref[...])
    g = (gate_b * _dwconv_rows(u, cw_ref, mask_ref, 0)).astype(BF16)
    acc_ref[...] += _dot(g, wo_ref[...])

    @pl.when(j == pl.num_programs(1) - 1)
    def _():
        x = jnp.where(is_lat, xl_ref[...], xc_ref[...])
        o_ref[...] = x + m_ref[2:3, :] * acc_ref[...]


def _short_conv_layer(x_lat, x_ctx, mods, layer, nw, w_in, conv_w, w_out, *, seq, ctx_row):
    t_lat, d = x_lat.shape
    t_ctx = x_ctx.shape[0]
    tm, tc = TOKEN_TILE, min(COL_TILE, d)
    n_lat, n_ctx, nj = t_lat // tm, t_ctx // tm, d // tc
    mod_row = _mod_row_fn(mods, seq, tm)
    kern = functools.partial(_sc_kernel, n_lat_tiles=n_lat, ctx_row=ctx_row)
    return pl.pallas_call(
        kern,
        grid=(n_lat + n_ctx, nj),
        in_specs=[
            pl.BlockSpec((tm, d), lambda i, j: (jnp.minimum(i, n_lat - 1), 0)),
            pl.BlockSpec((tm, d), lambda i, j: (jnp.maximum(i - n_lat, 0), 0)),
            pl.BlockSpec((None, None, 6, d), lambda i, j: (layer, mod_row(i), 0, 0)),
            pl.BlockSpec((1, d), lambda i, j: (0, 0)),
            pl.BlockSpec((d, tc), lambda i, j: (0, j)),
            pl.BlockSpec((d, tc), lambda i, j: (0, nj + j)),
            pl.BlockSpec((d, tc), lambda i, j: (0, 2 * nj + j)),
            pl.BlockSpec((conv_w.shape[0], tc), lambda i, j: (0, j)),
            pl.BlockSpec((tc, d), lambda i, j: (j, 0)),
        ],
        out_specs=pl.BlockSpec((tm, d), lambda i, j: (i, 0)),
        out_shape=jax.ShapeDtypeStruct((t_lat + t_ctx, d), F32),
        scratch_shapes=[pltpu.VMEM((tm, d), BF16), pltpu.VMEM((tm, d), F32),
                        pltpu.VMEM((conv_w.shape[0] - 1, tm, LANES), F32)],
        compiler_params=_params("arbitrary", "arbitrary"),
        name="short_conv_mixer",
    )(x_lat, x_ctx, mods, nw, w_in, w_in, w_in, conv_w, w_out)


def _route_kernel(x_ref, m_ref, nw_ref, wr_ref, br_ref, fp_ref, meta_ref, wts_ref, cnt_ref, run_ref,
                  *, n_groups, per_group):
    i = pl.program_id(0)

    @pl.when(i == 0)
    def _():
        run_ref[...] = jnp.zeros_like(run_ref)

    f = _mod_norm(x_ref[...], nw_ref[...], m_ref[3:4, :], m_ref[4:5, :])
    fp_ref[...] = _pack_row(f)
    logits = _dot(f.astype(BF16), wr_ref[...]) + br_ref[...]
    tm = logits.shape[0]
    lane = lax.broadcasted_iota(I32, logits.shape, 1).astype(F32)
    neg = -jnp.inf

    g_mask = lane < n_groups
    gl = jnp.where(g_mask, logits, neg)
    g_max = jnp.max(gl, axis=1, keepdims=True)
    g_sel = jnp.min(jnp.where(gl == g_max, lane, LANES), axis=1, keepdims=True)
    p_group = 1.0 / jnp.sum(jnp.where(g_mask, jnp.exp(gl - g_max), 0.0), axis=1, keepdims=True)

    e_lo = n_groups + g_sel * per_group
    in_group = (lane >= e_lo) & (lane < e_lo + per_group)
    el = jnp.where(in_group, logits, neg)
    e_max = jnp.max(el, axis=1, keepdims=True)
    ex = jnp.where(in_group, jnp.exp(el - e_max), 0.0)
    p = jnp.where(in_group, ex / jnp.sum(ex, axis=1, keepdims=True), -1.0)
    p1 = jnp.max(p, axis=1, keepdims=True)
    l1 = jnp.min(jnp.where(p == p1, lane, LANES), axis=1, keepdims=True)
    p_rest = jnp.where(lane == l1, -1.0, p)
    p2 = jnp.max(p_rest, axis=1, keepdims=True)
    l2 = jnp.min(jnp.where(p_rest == p2, lane, LANES), axis=1, keepdims=True)
    denom = p1 + p2
    w1 = p1 / denom * p_group
    w2 = p2 / denom * p_group

    hit1 = lane == l1
    hit2 = lane == l2
    onehot = jnp.where(hit1 | hit2, 1.0, 0.0)
    earlier = lax.broadcasted_iota(I32, (tm, tm), 0) > lax.broadcasted_iota(I32, (tm, tm), 1)
    before = _dot(jnp.where(earlier, 1.0, 0.0).astype(BF16), onehot.astype(BF16)) + run_ref[...]
    r1 = jnp.sum(jnp.where(hit1, before, 0.0), axis=1, keepdims=True)
    r2 = jnp.sum(jnp.where(hit2, before, 0.0), axis=1, keepdims=True)
    run_ref[...] += jnp.sum(onehot, axis=0, keepdims=True)
    cnt_ref[...] = run_ref[...].astype(I32)

    meta = jnp.where(lane == 0, l1 - n_groups,
                     jnp.where(lane == 1, l2 - n_groups, jnp.where(lane == 2, r1, r2)))
    meta_ref[...] = meta.astype(I32)
    wts_ref[...] = jnp.where(lane == 0, w1, w2)


def _moe_route(tokens, n_tok, mods, layer, nw, wr, br, *, seq, n_groups, per_group):
    d = tokens.shape[1]
    tm = TOKEN_TILE
    mod_row = _mod_row_fn(mods, seq, tm)
    kern = functools.partial(_route_kernel, n_groups=n_groups, per_group=per_group)
    return pl.pallas_call(
        kern,
        grid=(n_tok // tm,),
        in_specs=[
            pl.BlockSpec((tm, d), lambda i: (i, 0)),
            pl.BlockSpec((None, None, 6, d), lambda i: (layer, mod_row(i), 0, 0)),
            pl.BlockSpec((1, d), lambda i: (0, 0)),
            pl.BlockSpec((d, LANES), lambda i: (0, 0)),
            pl.BlockSpec((1, LANES), lambda i: (0, 0)),
        ],
        out_specs=[
            pl.BlockSpec((tm, d // 2), lambda i: (i, 0)),
            pl.BlockSpec((tm, LANES), lambda i: (i, 0)),
            pl.BlockSpec((tm, LANES), lambda i: (i, 0)),
            pl.BlockSpec((1, LANES), lambda i: (0, 0)),
        ],
        out_shape=[
            jax.ShapeDtypeStruct((n_tok, d // 2), U32),
            jax.ShapeDtypeStruct((n_tok, LANES), I32),
            jax.ShapeDtypeStruct((n_tok, LANES), F32),
            jax.ShapeDtypeStruct((1, LANES), I32),
        ],
        scratch_shapes=[pltpu.VMEM((1, LANES), F32)],
        compiler_params=_params("arbitrary"),
        name="moe_route",
    )(tokens, mods, nw, wr, br)


def _route_plan(meta, counts, *, n_tok, n_exp, n_groups, tg, tp):
    ids = meta[:, 0:TOP_K]
    rank = meta[:, TOP_K:2 * TOP_K]
    cnt = counts[0, n_groups:n_groups + n_exp]
    ends = jnp.cumsum(cnt)
    offs = ends - cnt
    pos = offs[ids] + rank
    pos_blocks = pos.T.reshape(TOP_K, n_tok // tp, tp).transpose(1, 0, 2)
    n_row_tiles = TOP_K * n_tok // tg
    first_tile = offs // tg
    n_it = jnp.where(cnt > 0, (ends - 1) // tg - first_tile + 1, 0)
    it_end = jnp.cumsum(n_it)
    it_start = it_end - n_it
    n_items = it_end[-1]
    w = jnp.minimum(jnp.arange(n_row_tiles + n_exp - 1, dtype=I32), n_items - 1)
    item_exp = jnp.sum(it_end[None, :] <= w[:, None], axis=1).astype(I32)
    item_tile = (first_tile[item_exp] + w - it_start[item_exp]).astype(I32)
    row_lo = jnp.maximum(offs[item_exp] - item_tile * tg, 0).astype(I32)
    row_hi = jnp.minimum(ends[item_exp] - item_tile * tg, tg).astype(I32)
    first = jnp.concatenate([jnp.ones((1,), I32), (item_tile[1:] != item_tile[:-1]).astype(I32)])
    return pos_blocks.astype(I32), (item_exp, item_tile, row_lo, row_hi, first, n_items.astype(I32).reshape(1))


def _perm_kernel(pos_ref, src_hbm, dst_hbm, sem, *, scatter, n_tok):
    i = pl.program_id(0)
    tp = pos_ref.shape[2]
    base = i * tp

    def row_copy(k, r):
        p = pos_ref[0, k, r]
        if scatter:
            return pltpu.make_async_copy(src_hbm.at[pl.ds(base + r, 1), :], dst_hbm.at[pl.ds(p, 1), :], sem.at[0])
        return pltpu.make_async_copy(src_hbm.at[pl.ds(p, 1), :], dst_hbm.at[pl.ds(k * n_tok + base + r, 1), :],
                                     sem.at[0])

    def body(r, c):
        for k in range(TOP_K):
            row_copy(k, r).start()
        return c
    lax.fori_loop(0, tp, body, 0, unroll=8)

    def wait_step():
        pltpu.make_async_copy(src_hbm.at[pl.ds(0, TOP_K * tp), :], dst_hbm.at[pl.ds(0, TOP_K * tp), :],
                              sem.at[0]).wait()

    @pl.when(i >= 1)
    def _():
        wait_step()

    @pl.when(i == pl.num_programs(0) - 1)
    def _():
        wait_step()


def _permute_rows(pos_blocks, src, *, scatter, n_tok, name):
    n_blocks, _, tp = pos_blocks.shape
    kern = functools.partial(_perm_kernel, scatter=scatter, n_tok=n_tok)
    return pl.pallas_call(
        kern,
        grid=(n_blocks,),
        in_specs=[pl.BlockSpec((1, TOP_K, tp), lambda i: (i, 0, 0), memory_space=pltpu.SMEM),
                  pl.BlockSpec(memory_space=pl.ANY)],
        out_specs=pl.BlockSpec(memory_space=pl.ANY),
        out_shape=jax.ShapeDtypeStruct((TOP_K * n_tok, src.shape[1]), src.dtype),
        scratch_shapes=[pltpu.SemaphoreType.DMA((1,))],
        compiler_params=_params("arbitrary"),
        name=name,
    )(pos_blocks, src)


def _gmm_kernel(ie_ref, it_ref, lo_ref, hi_ref, first_ref, n_ref, xs_ref, wg_ref, wu_ref, wd_ref, ys_ref,
                wg_b, wu_b, wd_b):
    w = pl.program_id(0)

    @pl.when(w < n_ref[0])
    def _():
        @pl.when((w == 0) | (ie_ref[w] != ie_ref[jnp.maximum(w - 1, 0)]))
        def _():
            wg_b[...] = wg_ref[...].astype(BF16)
            wu_b[...] = wu_ref[...].astype(BF16)
            wd_b[...] = wd_ref[...].astype(BF16)

        x_lo, x_hi = _unpack_pair(xs_ref[...])
        x_lo, x_hi = x_lo.astype(BF16), x_hi.astype(BF16)
        half = x_lo.shape[1]
        hg = _dot(x_lo, wg_b[0:half, :]) + _dot(x_hi, wg_b[half:, :])
        hu = _dot(x_lo, wu_b[0:half, :]) + _dot(x_hi, wu_b[half:, :])
        yp = _pack_row(_dot((_silu(hg) * hu).astype(BF16), wd_b[...]))
        rows = lax.broadcasted_iota(I32, yp.shape, 0)
        mine = (rows >= lo_ref[w]) & (rows < hi_ref[w])

        @pl.when(first_ref[w] == 1)
        def _():
            ys_ref[...] = jnp.where(mine, yp, jnp.uint32(0))

        @pl.when(first_ref[w] == 0)
        def _():
            ys_ref[...] = jnp.where(mine, yp, ys_ref[...])


def _moe_experts(xs, plan, wg, wu, wd):
    n_rows, half = xs.shape
    d = 2 * half
    de = wg.shape[2]
    tg = GROUP_TILE
    n_items_max = plan[0].shape[0]
    grid_spec = pltpu.PrefetchScalarGridSpec(
        num_scalar_prefetch=6,
        grid=(n_items_max,),
        in_specs=[
            pl.BlockSpec((tg, half), lambda w, ie, it, lo, hi, fi, n: (it[w], 0)),
            pl.BlockSpec((None, d, de), lambda w, ie, it, lo, hi, fi, n: (ie[w], 0, 0)),
            pl.BlockSpec((None, d, de), lambda w, ie, it, lo, hi, fi, n: (ie[w], 0, 0)),
            pl.BlockSpec((None, de, d), lambda w, ie, it, lo, hi, fi, n: (ie[w], 0, 0)),
        ],
        out_specs=pl.BlockSpec((tg, half), lambda w, ie, it, lo, hi, fi, n: (it[w], 0)),
        scratch_shapes=[pltpu.VMEM((d, de), BF16), pltpu.VMEM((d, de), BF16), pltpu.VMEM((de, d), BF16)],
    )
    return pl.pallas_call(
        _gmm_kernel,
        grid_spec=grid_spec,
        out_shape=jax.ShapeDtypeStruct((n_rows, half), U32),
        compiler_params=_params("arbitrary"),
        name="moe_experts",
    )(*plan, xs, wg, wu, wd)


def _moe_combine(y0_ref, y1_ref, wts_ref):
    wts = wts_ref[...]
    return wts[:, 0:1] * _unpack_row(y0_ref[...]) + wts[:, 1:2] * _unpack_row(y1_ref[...])


def _inproj_kernel(x_ref, y0_ref, y1_ref, wts_ref, mp_ref, m_ref, nw_ref, w_ref, wdt_ref, cw_ref, cb_ref, dtb_ref,
                   lat_ref, z_ref, xbc_ref, dt_ref, h_ref, mask_ref, *, nz, n_lat_tiles, ctx_row):
    i = pl.program_id(0)
    j = pl.program_id(1)
    n_sub = w_ref.shape[1] // COL_TILE

    @pl.when(j == 0)
    def _():
        lat = x_ref[...] + mp_ref[5:6, :] * _moe_combine(y0_ref, y1_ref, wts_ref)
        lat_ref[...] = lat
        h_ref[...] = _mod_norm(lat, nw_ref[...], m_ref[0:1, :], m_ref[1:2, :]).astype(BF16)
        _conv_masks(mask_ref, jnp.where(i < n_lat_tiles, GRID_W, ctx_row))

    @pl.when(j < nz)
    def _():
        for s in range(n_sub):
            sl = slice(s * COL_TILE, (s + 1) * COL_TILE)
            z_ref[:, sl] = _dot(h_ref[...], w_ref[:, sl]).astype(BF16)

    @pl.when(j >= nz)
    def _():
        for s in range(n_sub):
            sl = slice(s * COL_TILE, (s + 1) * COL_TILE)
            r = _dot(h_ref[...], w_ref[:, sl])
            xbc_ref[:, sl] = _silu(_dwconv_rows(r, cw_ref, mask_ref, s * COL_TILE) + cb_ref[:, sl]).astype(BF16)

    @pl.when(j == pl.num_programs(1) - 1)
    def _():
        v = _dot(h_ref[...], wdt_ref[...]) + dtb_ref[...]
        dt_ref[...] = jnp.maximum(v, 0.0) + jnp.log1p(jnp.exp(-jnp.abs(v)))


def _ssd_in_proj(tokens, y2, wts, mods, layer, nw, w_zx, w_dt, conv_w, conv_b, dt_bias,
                 *, seq, n_lat_tiles, ctx_row, d_inner, conv_dim):
    t, d = tokens.shape
    tm = TOKEN_TILE
    tc = next(c for c in (4 * COL_TILE, 2 * COL_TILE, COL_TILE) if d_inner % c == 0 and conv_dim % c == 0)
    nz, nx = d_inner // tc, conv_dim // tc
    mod_row = _mod_row_fn(mods, seq, tm)

    def xcol(j):
        return jnp.maximum(j - nz, 0)

    kern = functools.partial(_inproj_kernel, nz=nz, n_lat_tiles=n_lat_tiles, ctx_row=ctx_row)
    return pl.pallas_call(
        kern,
        grid=(t // tm, nz + nx),
        in_specs=[
            pl.BlockSpec((tm, d), lambda i, j: (i, 0)),
            pl.BlockSpec((None, tm, d // 2), lambda i, j: (0, i, 0)),
            pl.BlockSpec((None, tm, d // 2), lambda i, j: (1, i, 0)),
            pl.BlockSpec((tm, LANES), lambda i, j: (i, 0)),
            pl.BlockSpec((None, None, 6, d), lambda i, j: (layer - 1, mod_row(i), 0, 0)),
            pl.BlockSpec((None, None, 6, d), lambda i, j: (layer, mod_row(i), 0, 0)),
            pl.BlockSpec((1, d), lambda i, j: (0, 0)),
            pl.BlockSpec((d, tc), lambda i, j: (0, j)),
            pl.BlockSpec((d, LANES), lambda i, j: (0, 0)),
            pl.BlockSpec((conv_w.shape[0], tc), lambda i, j: (0, xcol(j))),
            pl.BlockSpec((1, tc), lambda i, j: (0, xcol(j))),
            pl.BlockSpec((1, LANES), lambda i, j: (0, 0)),
        ],
        out_specs=[
            pl.BlockSpec((tm, d), lambda i, j: (i, 0)),
            pl.BlockSpec((tm, tc), lambda i, j: (i, jnp.minimum(j, nz - 1))),
            pl.BlockSpec((tm, tc), lambda i, j: (i, xcol(j))),
            pl.BlockSpec((tm, LANES), lambda i, j: (i, 0)),
        ],
        out_shape=[
            jax.ShapeDtypeStruct((t, d), F32),
            jax.ShapeDtypeStruct((t, d_inner), BF16),
            jax.ShapeDtypeStruct((t, conv_dim), BF16),
            jax.ShapeDtypeStruct((t, LANES), F32),
        ],
        scratch_shapes=[pltpu.VMEM((tm, d), BF16), pltpu.VMEM((conv_w.shape[0] - 1, tm, LANES), F32)],
        compiler_params=_params("arbitrary", "arbitrary"),
        name="ssd_in_proj",
    )(tokens, y2, y2, wts, mods, mods, nw, w_zx, w_dt, conv_w, conv_b, dt_bias)


def _ssd_kernel(x_ref, b_ref, c_ref, dt_ref, a_ref, y_ref, st_ref, *, n_heads):
    d = pl.program_id(1)
    k = pl.program_id(2)
    q = x_ref.shape[0]
    n_groups = st_ref.shape[0]
    gw = st_ref.shape[2]
    hpg = gw // HEAD_DIM
    pw = 2 * HEAD_DIM
    fwd = d == 0

    @pl.when(k == 0)
    def _():
        st_ref[...] = jnp.zeros_like(st_ref)

    t_i = lax.broadcasted_iota(I32, (q, q), 0)
    s_i = lax.broadcasted_iota(I32, (q, q), 1)
    incl = jnp.where(fwd, t_i - s_i, s_i - t_i) >= 0

    dt_all = dt_ref[...]
    dta_all = dt_all * a_ref[...]
    cum_all = jnp.dot(jnp.where(incl, 1.0, 0.0), dta_all, preferred_element_type=F32,
                      precision=lax.Precision.HIGHEST)
    cum_t_all = cum_all.T
    dt_t_all = dt_all.T
    tot_t_all = jnp.sum(dta_all.T, axis=1, keepdims=True)

    def pick_cols(a):
        return jnp.where(fwd, a[:, 0:n_heads], a[:, n_heads:2 * n_heads])

    def pick_rows(a):
        return jnp.where(fwd, a[0:n_heads, :], a[n_heads:2 * n_heads, :])

    cum = pick_cols(cum_all)
    tot = pick_cols(jnp.sum(dta_all, axis=0, keepdims=True))
    cum_t = pick_rows(cum_t_all)
    dt_t = pick_rows(dt_t_all)
    to_end_t = dt_t * jnp.exp(pick_rows(tot_t_all) - cum_t)
    dec_all = jnp.exp(tot)

    low_half = lax.broadcasted_iota(I32, (q, pw), 1) < HEAD_DIM
    low_row = lax.broadcasted_iota(I32, (1, pw), 1) < HEAD_DIM

    for g in range(n_groups):
        bg = b_ref[:, g * D_STATE:(g + 1) * D_STATE]
        cg = c_ref[:, g * D_STATE:(g + 1) * D_STATE]
        cb = lax.dot_general(cg, bg, (((1,), (1,)), ((), ())), preferred_element_type=F32)
        cb = jnp.where(incl, cb, 0.0)
        bg_t = bg.astype(F32).T
        st = st_ref[g]
        y_off = _dot(cg, st.astype(BF16))

        for pair in range(hpg // 2):
            psl = slice(pair * pw, (pair + 1) * pw)
            xp = x_ref[:, g * gw + pair * pw:g * gw + (pair + 1) * pw]
            zero = jnp.zeros_like(xp)
            rhs = jnp.concatenate([jnp.where(low_half, xp, zero), jnp.where(low_half, zero, xp)], axis=0)
            w_parts, u_parts, cols = [], [], []
            for hh in range(2):
                h = g * hpg + pair * 2 + hh
                col = jnp.broadcast_to(cum[:, h:h + 1], (q, q))
                seg = jnp.exp(jnp.minimum(col - cum_t[h:h + 1, :], 0.0))
                w_parts.append((cb * seg * dt_t[h:h + 1, :]).astype(BF16))
                u_parts.append((bg_t * to_end_t[h:h + 1, :]).astype(BF16))
                cols.append(col)
            y_diag = _dot(jnp.concatenate(w_parts, axis=1), rhs)
            upd = _dot(jnp.concatenate(u_parts, axis=1), rhs)
            dec_in = jnp.exp(jnp.where(low_half, cols[0], cols[1]))
            y_ref[:, g * gw + pair * pw:g * gw + (pair + 1) * pw] = (y_diag + y_off[:, psl] * dec_in).astype(BF16)
            h0 = g * hpg + pair * 2
            dec_pair = jnp.where(low_row, jnp.broadcast_to(dec_all[:, h0:h0 + 1], (1, pw)),
                                 jnp.broadcast_to(dec_all[:, h0 + 1:h0 + 2], (1, pw)))
            st_ref[g, :, psl] = st[:, psl] * dec_pair + upd


def _ssd_scan(xbc, dt, a_neg, *, n_batch, lat_chunks, ctx_chunks, d_inner, n_heads):
    t = xbc.shape[0]
    q = CHUNK
    gn = SSD_GROUPS * D_STATE
    gw = d_inner // SSD_GROUPS
    assert gw % (2 * HEAD_DIM) == 0 and d_inner % gn == 0 and q == 2 * HEAD_DIM and q == LANES
    n_steps = ctx_chunks + lat_chunks
    ctx_base = n_batch * lat_chunks

    def chunk(b, d, k):
        kc = jnp.where(d == 0, k, ctx_chunks - 1 - k)
        kl = jnp.where(d == 0, k - ctx_chunks, n_steps - 1 - k)
        return jnp.where(k < ctx_chunks, ctx_base + b * ctx_chunks + kc, b * lat_chunks + kl)

    kern = functools.partial(_ssd_kernel, n_heads=n_heads)
    return pl.pallas_call(
        kern,
        grid=(n_batch, 2, n_steps),
        in_specs=[
            pl.BlockSpec((q, d_inner), lambda b, d, k: (chunk(b, d, k), 0)),
            pl.BlockSpec((q, gn), lambda b, d, k: (chunk(b, d, k), d_inner // gn)),
            pl.BlockSpec((q, gn), lambda b, d, k: (chunk(b, d, k), d_inner // gn + 1)),
            pl.BlockSpec((q, LANES), lambda b, d, k: (chunk(b, d, k), 0)),
            pl.BlockSpec((1, LANES), lambda b, d, k: (0, 0)),
        ],
        out_specs=pl.BlockSpec((None, q, d_inner), lambda b, d, k: (d, chunk(b, d, k), 0)),
        out_shape=jax.ShapeDtypeStruct((2, t, d_inner), BF16),
        scratch_shapes=[pltpu.VMEM((SSD_GROUPS, D_STATE, gw), F32)],
        compiler_params=_params("arbitrary", "arbitrary", "arbitrary"),
        name="ssd_scan",
    )(xbc, xbc, xbc, dt, a_neg)


def _outproj_kernel(yf_ref, yb_ref, xs_ref, z_ref, dsk_ref, gnw_ref, lat_ref, m_ref, w_ref, o_ref):
    gw = w_ref.shape[0] // SSD_GROUPS
    acc = None
    for g in range(SSD_GROUPS):
        sl = slice(g * gw, (g + 1) * gw)
        y = (yf_ref[:, sl].astype(F32) + yb_ref[:, sl].astype(F32)
             + dsk_ref[:, sl] * xs_ref[:, sl].astype(F32))
        gated = y * _silu(z_ref[:, sl].astype(F32))
        ms = jnp.mean(gated * gated, axis=-1, keepdims=True)
        gn = (gated * lax.rsqrt(ms + EPS) * gnw_ref[:, sl]).astype(BF16)
        part = _dot(gn, w_ref[sl, :])
        acc = part if acc is None else acc + part
    o_ref[...] = lat_ref[...] + m_ref[2:3, :] * acc


def _ssd_out_proj(y, xbc, z, d_skip, gnw, tokens, mods, layer, w_out, *, n_tok, seq):
    d_inner, d = w_out.shape
    tm = OUT_TOKEN_TILE
    mod_row = _mod_row_fn(mods, seq, tm)
    return pl.pallas_call(
        _outproj_kernel,
        grid=(n_tok // tm,),
        in_specs=[
            pl.BlockSpec((None, tm, d_inner), lambda i: (0, i, 0)),
            pl.BlockSpec((None, tm, d_inner), lambda i: (1, i, 0)),
            pl.BlockSpec((tm, d_inner), lambda i: (i, 0)),
            pl.BlockSpec((tm, d_inner), lambda i: (i, 0)),
            pl.BlockSpec((1, d_inner), lambda i: (0, 0)),
            pl.BlockSpec((1, d_inner), lambda i: (0, 0)),
            pl.BlockSpec((tm, d), lambda i: (i, 0)),
            pl.BlockSpec((None, None, 6, d), lambda i: (layer, mod_row(i), 0, 0)),
            pl.BlockSpec((d_inner, d), lambda i: (0, 0), pipeline_mode=pl.Buffered(1)),
        ],
        out_specs=pl.BlockSpec((tm, d), lambda i: (i, 0)),
        out_shape=jax.ShapeDtypeStruct((n_tok, d), F32),
        compiler_params=_params("arbitrary"),
        name="ssd_out_proj",
    )(y, y, xbc, z, d_skip, gnw, tokens, mods, w_out)


def _final_kernel(x_ref, y0_ref, y1_ref, wts_ref, m_ref, nw_ref, o_ref):
    lat = x_ref[...] + m_ref[5:6, :] * _moe_combine(y0_ref, y1_ref, wts_ref)
    ms = jnp.mean(lat * lat, axis=-1, keepdims=True)
    o_ref[...] = lat * lax.rsqrt(ms + EPS) * nw_ref[...]


def _final_norm(tokens, y2, wts, mods, layer, nw, *, seq):
    t, d = tokens.shape
    tm = TOKEN_TILE
    mod_row = _mod_row_fn(mods, seq, tm)
    return pl.pallas_call(
        _final_kernel,
        grid=(t // tm,),
        in_specs=[
            pl.BlockSpec((tm, d), lambda i: (i, 0)),
            pl.BlockSpec((None, tm, d // 2), lambda i: (0, i, 0)),
            pl.BlockSpec((None, tm, d // 2), lambda i: (1, i, 0)),
            pl.BlockSpec((tm, LANES), lambda i: (i, 0)),
            pl.BlockSpec((None, None, 6, d), lambda i: (layer, mod_row(i), 0, 0)),
            pl.BlockSpec((1, d), lambda i: (0, 0)),
        ],
        out_specs=pl.BlockSpec((tm, d), lambda i: (i, 0)),
        out_shape=jax.ShapeDtypeStruct((t, d), F32),
        compiler_params=_params("arbitrary"),
        name="final_norm",
    )(tokens, y2, y2, wts, mods, nw)


def _router_weights(rg_w, rg_b, re_w, re_b):
    d, n_groups = rg_w.shape
    n_exp = re_w.shape[1]
    pad = LANES - n_groups - n_exp
    wr = jnp.concatenate([rg_w, re_w, jnp.zeros((d, pad), F32)], axis=1).astype(BF16)
    br = jnp.concatenate([rg_b, re_b, jnp.zeros((pad,), F32)]).reshape(1, LANES)
    return wr, br


def _moe(tokens, n_tok, mods, layer, nw, rg_w, rg_b, re_w, re_b, w_gate, w_up, w_down, *, seq):
    n_groups = rg_w.shape[1]
    n_exp = re_w.shape[1]
    assert n_tok % PERM_TILE == 0 and (TOP_K * n_tok) % GROUP_TILE == 0
    wr, br = _router_weights(rg_w, rg_b, re_w, re_b)
    fp, meta, wts, counts = _moe_route(tokens, n_tok, mods, layer, nw, wr, br, seq=seq, n_groups=n_groups,
                                       per_group=n_exp // n_groups)
    pos_blocks, plan = _route_plan(meta, counts, n_tok=n_tok, n_exp=n_exp, n_groups=n_groups,
                                   tg=GROUP_TILE, tp=PERM_TILE)
    xs = _permute_rows(pos_blocks, fp, scatter=True, n_tok=n_tok, name="moe_sort_rows")
    ys = _moe_experts(xs, plan, w_gate, w_up, w_down)
    y2 = _permute_rows(pos_blocks, ys, scatter=False, n_tok=n_tok, name="moe_unsort_rows")
    return y2.reshape(TOP_K, n_tok, y2.shape[1]), wts


def kernel(x, c, ctx, c_ctx, ada_w, ada_b, norm_mix_w, norm_ffn_w, sc_w_in, sc_conv_w, sc_w_out, ssd_w_in,
           ssd_conv_w, ssd_conv_b, ssd_dt_bias, ssd_a_log, ssd_d, ssd_norm_w, ssd_w_out, rg_w, rg_b, re_w, re_b,
           moe_w_gate, moe_w_up, moe_w_down, final_norm_w):
    n_batch, seq, d = x.shape
    ctx_len = ctx.shape[1]
    depth = ada_w.shape[0]
    d_inner = ssd_w_out.shape[1]
    conv_dim = ssd_conv_w.shape[2]
    n_heads = ssd_d.shape[1]
    tm = TOKEN_TILE
    t_lat, t_ctx = n_batch * seq, n_batch * ctx_len
    assert depth == 2 and seq % tm == 0 and t_ctx % tm == 0 and tm % ctx_len == 0 and tm % GRID_W == 0
    assert ctx_len & (ctx_len - 1) == 0 and GRID_W & (GRID_W - 1) == 0
    assert seq % CHUNK == 0 and ctx_len % CHUNK == 0 and d_inner == n_heads * HEAD_DIM
    assert n_batch + 1 <= SUBLANES and 2 * n_heads <= LANES and d_inner % COL_TILE == 0 and conv_dim % COL_TILE == 0
    n_lat_tiles = t_lat // tm

    cvec = jnp.zeros((SUBLANES, d), F32).at[0:n_batch].set(c).at[n_batch].set(c_ctx)
    mods = _ada_mods(cvec, ada_w, ada_b)

    x_lat = x.reshape(t_lat, d)
    x_ctx = ctx.reshape(t_ctx, d)

    tokens = _short_conv_layer(x_lat, x_ctx, mods, 0, norm_mix_w[0:1], sc_w_in[0].astype(BF16), sc_conv_w[0],
                               sc_w_out[0].astype(BF16), seq=seq, ctx_row=ctx_len)
    t_all = t_lat + t_ctx
    y2, wts = _moe(tokens, t_all, mods, 0, norm_ffn_w[0:1], rg_w[0], rg_b[0], re_w[0], re_b[0],
                   moe_w_gate[0], moe_w_up[0], moe_w_down[0], seq=seq)

    w_in = ssd_w_in[0]
    dt_cols = 2 * n_heads
    w_zx = w_in[:, :d_inner + conv_dim].astype(BF16)
    w_dt = jnp.concatenate([w_in[:, d_inner + conv_dim:], jnp.zeros((d, LANES - dt_cols), F32)],
                           axis=1).astype(BF16)
    dt_bias = jnp.zeros((1, LANES), F32).at[0, :dt_cols].set(ssd_dt_bias[0].reshape(-1))
    a_neg = jnp.zeros((1, LANES), F32).at[0, :dt_cols].set(-jnp.exp(ssd_a_log[0].reshape(-1)))
    tokens, z, xbc, dt = _ssd_in_proj(tokens, y2, wts, mods, 1, norm_mix_w[1:2], w_zx, w_dt, ssd_conv_w[0],
                                      ssd_conv_b[0].reshape(1, conv_dim), dt_bias, seq=seq,
                                      n_lat_tiles=n_lat_tiles, ctx_row=ctx_len, d_inner=d_inner, conv_dim=conv_dim)
    y = _ssd_scan(xbc, dt, a_neg, n_batch=n_batch, lat_chunks=seq // CHUNK, ctx_chunks=ctx_len // CHUNK,
                  d_inner=d_inner, n_heads=n_heads)
    d_skip = jnp.repeat(ssd_d[0], HEAD_DIM).reshape(1, d_inner)
    lat = _ssd_out_proj(y, xbc, z, d_skip, ssd_norm_w[0].reshape(1, d_inner), tokens, mods, 1,
                        ssd_w_out[0].astype(BF16), n_tok=t_lat, seq=seq)

    y2, wts = _moe(lat, t_lat, mods, 1, norm_ffn_w[1:2], rg_w[1], rg_b[1], re_w[1], re_b[1],
                   moe_w_gate[1], moe_w_up[1], moe_w_down[1], seq=seq)
    out = _final_norm(lat, y2, wts, mods, 1, final_norm_w.reshape(1, d), seq=seq)
    return out.reshape(n_batch, seq, d)
```

```python
import functools

import jax
import jax.numpy as jnp
from jax import lax
from jax.experimental import pallas as pl
from jax.experimental.pallas import tpu as pltpu

F32 = jnp.float32
BF16 = jnp.bfloat16
I32 = jnp.int32
U32 = jnp.uint32

EPS = 1e-6
GRID_W = 64
HEAD_DIM = 64
D_STATE = 128
SSD_GROUPS = 8
CHUNK = 128
TOP_K = 2

LANES = 128
SUBLANES = 8
VMEM_LIMIT = 58 * 1024 * 1024
TOKEN_TILE = 512
COL_TILE = 512
OUT_TOKEN_TILE = 256
GROUP_TILE = 256
PERM_TILE = 512
ADA_COL_TILE = 1024
HI_MASK = 0xFFFF0000
LOG2_E = 1.4426950408889634


def _params(*sem):
    return pltpu.CompilerParams(dimension_semantics=sem, vmem_limit_bytes=VMEM_LIMIT)


def _dot(a, b):
    return jnp.dot(a, b, preferred_element_type=F32)


def _silu(x):
    return x * jax.nn.sigmoid(x)


def _mod_norm(x, nw, shift, scale):
    ms = jnp.mean(x * x, axis=-1, keepdims=True)
    y = x * lax.rsqrt(ms + EPS)
    return (y * nw) * (1.0 + scale) + shift


def _pack_pair(lo, hi):
    lo_bits = lax.bitcast_convert_type(lo.astype(BF16).astype(F32), U32)
    hi_bits = lax.bitcast_convert_type(hi.astype(BF16).astype(F32), U32)
    return (lo_bits >> 16) | (hi_bits & jnp.uint32(HI_MASK))


def _unpack_pair(u):
    lo = lax.bitcast_convert_type(u << 16, F32)
    hi = lax.bitcast_convert_type(u & jnp.uint32(HI_MASK), F32)
    return lo, hi


def _pack_row(x):
    half = x.shape[1] // 2
    return _pack_pair(x[:, :half], x[:, half:])


def _unpack_row(u):
    lo, hi = _unpack_pair(u)
    return jnp.concatenate([lo, hi], axis=1)


def _store_token_tiles(ref, packed):
    rows, width = packed.shape
    s = width // LANES
    for c in range(s):
        ref[pl.ds(c, rows, stride=s), :] = packed[:, c * LANES:(c + 1) * LANES]


def _load_token_tiles(ref, s):
    rows = ref.shape[0] // s
    return jnp.concatenate([ref[pl.ds(c, rows, stride=s), :] for c in range(s)], axis=1)


def _conv_masks(mask_ref, row_len):
    n_side, tm, _ = mask_ref.shape
    half = n_side // 2
    pos = lax.broadcasted_iota(I32, (tm, LANES), 0) & (row_len - 1)
    for m in range(n_side):
        off = m - half if m < half else m - half + 1
        valid = (pos + off >= 0) & (pos + off < row_len)
        mask_ref[m] = jnp.where(valid, 1.0, 0.0)


def _dwconv_rows(u, cw_ref, mask_ref, col0):
    tm, width_c = u.shape
    width = cw_ref.shape[0]
    half = width // 2
    outs = []
    for c in range(width_c // LANES):
        sl = slice(c * LANES, (c + 1) * LANES)
        wsl = slice(col0 + c * LANES, col0 + (c + 1) * LANES)
        uc = u[:, sl]
        out = cw_ref[half:half + 1, wsl] * uc
        for k in range(width):
            off = k - half
            if off == 0:
                continue
            m = k if k < half else k - 1
            shifted = pltpu.roll(uc, (-off) % tm, axis=0)
            out = out + cw_ref[k:k + 1, wsl] * (shifted * mask_ref[m])
        outs.append(out)
    return outs[0] if len(outs) == 1 else jnp.concatenate(outs, axis=1)


def _ada_kernel(c_ref, w_ref, b_ref, o_ref):
    s = _silu(c_ref[...])
    o_ref[...] = jnp.dot(s, w_ref[...], preferred_element_type=F32,
                         precision=lax.Precision.HIGHEST) + b_ref[...]


def _ada_mods(cvec, ada_w, ada_b):
    depth, d, d6 = ada_w.shape
    tn = min(ADA_COL_TILE, d6)
    out = pl.pallas_call(
        _ada_kernel,
        grid=(depth, d6 // tn),
        in_specs=[pl.BlockSpec((SUBLANES, d), lambda l, j: (0, 0)),
                  pl.BlockSpec((None, d, tn), lambda l, j: (l, 0, j)),
                  pl.BlockSpec((None, 1, tn), lambda l, j: (l, 0, j))],
        out_specs=pl.BlockSpec((None, SUBLANES, tn), lambda l, j: (l, 0, j)),
        out_shape=jax.ShapeDtypeStruct((depth, SUBLANES, d6), F32),
        compiler_params=_params("arbitrary", "arbitrary"),
        name="ada_mods",
    )(cvec, ada_w, ada_b.reshape(depth, 1, d6))
    return out.reshape(depth, SUBLANES, 6, d)


def _mod_row_fn(mods, tokens_per_batch, tile):
    n_rows = mods.shape[1]
    per_batch = tokens_per_batch // tile

    def mod_row(i):
        return jnp.minimum(i // per_batch, n_rows - 1)
    return mod_row


def _sc_kernel(xl_ref, xc_ref, m_ref, nw_ref, wb_ref, wc_ref, wv_ref, cw_ref, wo_ref,
               o_ref, h_ref, acc_ref, mask_ref, *, n_lat_tiles, ctx_row):
    i = pl.program_id(0)
    j = pl.program_id(1)
    is_lat = i < n_lat_tiles

    @pl.when(j == 0)
    def _():
        x = jnp.where(is_lat, xl_ref[...], xc_ref[...])
        h_ref[...] = _mod_norm(x, nw_ref[...], m_ref[0:1, :], m_ref[1:2, :]).astype(BF16)
        acc_ref[...] = jnp.zeros_like(acc_ref)
        _conv_masks(mask_ref, jnp.where(is_lat, GRID_W, ctx_row))

    h = h_ref[...]
    gate_b = _dot(h, wb_ref[...])
    u = _dot(h, wc_ref[...]) * _dot(h, wv_ref[...])
    g = (gate_b * _dwconv_rows(u, cw_ref, mask_ref, 0)).astype(BF16)
    acc_ref[...] += _dot(g, wo_ref[...])

    @pl.when(j == pl.num_programs(1) - 1)
    def _():
        x = jnp.where(is_lat, xl_ref[...], xc_ref[...])
        o_ref[...] = x + m_ref[2:3, :] * acc_ref[...]


def _short_conv_layer(x_lat, x_ctx, mods, layer, nw, w_in, conv_w, w_out, *, seq, ctx_row):
    t_lat, d = x_lat.shape
    t_ctx = x_ctx.shape[0]
    tm, tc = TOKEN_TILE, min(COL_TILE, d)
    n_lat, n_ctx, nj = t_lat // tm, t_ctx // tm, d // tc
    mod_row = _mod_row_fn(mods, seq, tm)
    kern = functools.partial(_sc_kernel, n_lat_tiles=n_lat, ctx_row=ctx_row)
    return pl.pallas_call(
        kern,
        grid=(n_lat + n_ctx, nj),
        in_specs=[
            pl.BlockSpec((tm, d), lambda i, j: (jnp.minimum(i, n_lat - 1), 0)),
            pl.BlockSpec((tm, d), lambda i, j: (jnp.maximum(i - n_lat, 0), 0)),
            pl.BlockSpec((None, None, 6, d), lambda i, j: (layer, mod_row(i), 0, 0)),
            pl.BlockSpec((1, d), lambda i, j: (0, 0)),
            pl.BlockSpec((d, tc), lambda i, j: (0, j)),
            pl.BlockSpec((d, tc), lambda i, j: (0, nj + j)),
            pl.BlockSpec((d, tc), lambda i, j: (0, 2 * nj + j)),
            pl.BlockSpec((conv_w.shape[0], tc), lambda i, j: (0, j)),
            pl.BlockSpec((tc, d), lambda i, j: (j, 0)),
        ],
        out_specs=pl.BlockSpec((tm, d), lambda i, j: (i, 0)),
        out_shape=jax.ShapeDtypeStruct((t_lat + t_ctx, d), F32),
        scratch_shapes=[pltpu.VMEM((tm, d), BF16), pltpu.VMEM((tm, d), F32),
                        pltpu.VMEM((conv_w.shape[0] - 1, tm, LANES), F32)],
        compiler_params=_params("arbitrary", "arbitrary"),
        name="short_conv_mixer",
    )(x_lat, x_ctx, mods, nw, w_in, w_in, w_in, conv_w, w_out)


def _route_kernel(x_ref, m_ref, nw_ref, wr_ref, br_ref, fp_ref, meta_ref, wts_ref, cnt_ref, run_ref,
                  *, n_groups, per_group):
    i = pl.program_id(0)

    @pl.when(i == 0)
    def _():
        run_ref[...] = jnp.zeros_like(run_ref)

    f = _mod_norm(x_ref[...], nw_ref[...], m_ref[3:4, :], m_ref[4:5, :])
    _store_token_tiles(fp_ref, _pack_row(f))
    logits = _dot(f.astype(BF16), wr_ref[...]) + br_ref[...]
    tm = logits.shape[0]
    lane = lax.broadcasted_iota(I32, logits.shape, 1).astype(F32)
    neg = -jnp.inf

    g_mask = lane < n_groups
    gl = jnp.where(g_mask, logits, neg)
    g_max = jnp.max(gl, axis=1, keepdims=True)
    g_sel = jnp.min(jnp.where(gl == g_max, lane, LANES), axis=1, keepdims=True)
    p_group = 1.0 / jnp.sum(jnp.where(g_mask, jnp.exp(gl - g_max), 0.0), axis=1, keepdims=True)

    e_lo = n_groups + g_sel * per_group
    in_group = (lane >= e_lo) & (lane < e_lo + per_group)
    el = jnp.where(in_group, logits, neg)
    e_max = jnp.max(el, axis=1, keepdims=True)
    ex = jnp.where(in_group, jnp.exp(el - e_max), 0.0)
    p = jnp.where(in_group, ex / jnp.sum(ex, axis=1, keepdims=True), -1.0)
    p1 = jnp.max(p, axis=1, keepdims=True)
    l1 = jnp.min(jnp.where(p == p1, lane, LANES), axis=1, keepdims=True)
    p_rest = jnp.where(lane == l1, -1.0, p)
    p2 = jnp.max(p_rest, axis=1, keepdims=True)
    l2 = jnp.min(jnp.where(p_rest == p2, lane, LANES), axis=1, keepdims=True)
    denom = p1 + p2
    w1 = p1 / denom * p_group
    w2 = p2 / denom * p_group

    hit1 = lane == l1
    hit2 = lane == l2
    onehot = jnp.where(hit1 | hit2, 1.0, 0.0)
    earlier = lax.broadcasted_iota(I32, (tm, tm), 0) > lax.broadcasted_iota(I32, (tm, tm), 1)
    before = _dot(jnp.where(earlier, 1.0, 0.0).astype(BF16), onehot.astype(BF16)) + run_ref[...]
    r1 = jnp.sum(jnp.where(hit1, before, 0.0), axis=1, keepdims=True)
    r2 = jnp.sum(jnp.where(hit2, before, 0.0), axis=1, keepdims=True)
    run_ref[...] += jnp.sum(onehot, axis=0, keepdims=True)
    cnt_ref[...] = run_ref[...].astype(I32)

    meta = jnp.where(lane == 0, l1 - n_groups,
                     jnp.where(lane == 1, l2 - n_groups, jnp.where(lane == 2, r1, r2)))
    meta_ref[...] = meta.astype(I32)
    wts_ref[...] = jnp.where(lane == 0, w1, w2)


def _moe_route(tokens, n_tok, mods, layer, nw, wr, br, *, seq, n_groups, per_group):
    d = tokens.shape[1]
    tm = TOKEN_TILE
    s = d // 2 // LANES
    mod_row = _mod_row_fn(mods, seq, tm)
    kern = functools.partial(_route_kernel, n_groups=n_groups, per_group=per_group)
    return pl.pallas_call(
        kern,
        grid=(n_tok // tm,),
        in_specs=[
            pl.BlockSpec((tm, d), lambda i: (i, 0)),
            pl.BlockSpec((None, None, 6, d), lambda i: (layer, mod_row(i), 0, 0)),
            pl.BlockSpec((1, d), lambda i: (0, 0)),
            pl.BlockSpec((d, LANES), lambda i: (0, 0)),
            pl.BlockSpec((1, LANES), lambda i: (0, 0)),
        ],
        out_specs=[
            pl.BlockSpec((tm * s, LANES), lambda i: (i, 0)),
            pl.BlockSpec((tm, LANES), lambda i: (i, 0)),
            pl.BlockSpec((tm, LANES), lambda i: (i, 0)),
            pl.BlockSpec((1, LANES), lambda i: (0, 0)),
        ],
        out_shape=[
            jax.ShapeDtypeStruct((n_tok * s, LANES), U32),
            jax.ShapeDtypeStruct((n_tok, LANES), I32),
            jax.ShapeDtypeStruct((n_tok, LANES), F32),
            jax.ShapeDtypeStruct((1, LANES), I32),
        ],
        scratch_shapes=[pltpu.VMEM((1, LANES), F32)],
        compiler_params=_params("arbitrary"),
        name="moe_route",
    )(tokens, mods, nw, wr, br)


def _route_plan(meta, counts, *, n_tok, n_exp, n_groups, tg, tp):
    ids = meta[:, 0:TOP_K]
    rank = meta[:, TOP_K:2 * TOP_K]
    cnt = counts[0, n_groups:n_groups + n_exp]
    ends = jnp.cumsum(cnt)
    offs = ends - cnt
    pos = offs[ids] + rank
    pos_blocks = pos.T.reshape(TOP_K, n_tok // tp, tp).transpose(1, 0, 2)
    n_row_tiles = TOP_K * n_tok // tg
    first_tile = offs // tg
    n_it = jnp.where(cnt > 0, (ends - 1) // tg - first_tile + 1, 0)
    it_end = jnp.cumsum(n_it)
    it_start = it_end - n_it
    n_items = it_end[-1]
    w = jnp.minimum(jnp.arange(n_row_tiles + n_exp - 1, dtype=I32), n_items - 1)
    item_exp = jnp.sum(it_end[None, :] <= w[:, None], axis=1).astype(I32)
    item_tile = (first_tile[item_exp] + w - it_start[item_exp]).astype(I32)
    row_lo = jnp.maximum(offs[item_exp] - item_tile * tg, 0).astype(I32)
    row_hi = jnp.minimum(ends[item_exp] - item_tile * tg, tg).astype(I32)
    first = jnp.concatenate([jnp.ones((1,), I32), (item_tile[1:] != item_tile[:-1]).astype(I32)])
    return pos_blocks.astype(I32), (item_exp, item_tile, row_lo, row_hi, first, n_items.astype(I32).reshape(1))


def _token_copy(src, src_tok, dst, dst_tok, s, sem):
    return pltpu.make_async_copy(src.at[pl.ds(pl.multiple_of(src_tok * s, s), s), :],
                                 dst.at[pl.ds(pl.multiple_of(dst_tok * s, s), s), :], sem.at[0])


def _sort_kernel(pos_ref, src_ref, dst_hbm, sem, *, s):
    tp = pos_ref.shape[2]

    def body(r, c):
        for k in range(TOP_K):
            _token_copy(src_ref, r, dst_hbm, pos_ref[0, k, r], s, sem).start()
        return c
    lax.fori_loop(0, tp, body, 0, unroll=8)
    for k in range(TOP_K):
        pltpu.make_async_copy(src_ref, dst_hbm.at[pl.ds(0, tp * s), :], sem.at[0]).wait()


def _unsort_kernel(pos_ref, src_hbm, o_ref, sem, *, s):
    tp = pos_ref.shape[2]

    def body(r, c):
        for k in range(TOP_K):
            _token_copy(src_hbm, pos_ref[0, k, r], o_ref.at[k], r, s, sem).start()
        return c
    lax.fori_loop(0, tp, body, 0, unroll=8)
    for k in range(TOP_K):
        pltpu.make_async_copy(src_hbm.at[pl.ds(0, tp * s), :], o_ref.at[k], sem.at[0]).wait()


def _sort_rows(pos_blocks, src, *, n_tok, s):
    n_blocks, _, tp = pos_blocks.shape
    return pl.pallas_call(
        functools.partial(_sort_kernel, s=s),
        grid=(n_blocks,),
        in_specs=[pl.BlockSpec((1, TOP_K, tp), lambda i: (i, 0, 0), memory_space=pltpu.SMEM),
                  pl.BlockSpec((tp * s, LANES), lambda i: (i, 0))],
        out_specs=pl.BlockSpec(memory_space=pl.ANY),
        out_shape=jax.ShapeDtypeStruct((TOP_K * n_tok * s, LANES), src.dtype),
        scratch_shapes=[pltpu.SemaphoreType.DMA((1,))],
        compiler_params=_params("arbitrary"),
        name="moe_sort_rows",
    )(pos_blocks, src)


def _unsort_rows(pos_blocks, src, *, n_tok, s):
    n_blocks, _, tp = pos_blocks.shape
    return pl.pallas_call(
        functools.partial(_unsort_kernel, s=s),
        grid=(n_blocks,),
        in_specs=[pl.BlockSpec((1, TOP_K, tp), lambda i: (i, 0, 0), memory_space=pltpu.SMEM),
                  pl.BlockSpec(memory_space=pl.ANY)],
        out_specs=pl.BlockSpec((TOP_K, tp * s, LANES), lambda i: (0, i, 0)),
        out_shape=jax.ShapeDtypeStruct((TOP_K, n_tok * s, LANES), src.dtype),
        scratch_shapes=[pltpu.SemaphoreType.DMA((1,))],
        compiler_params=_params("arbitrary"),
        name="moe_unsort_rows",
    )(pos_blocks, src)


def _gmm_kernel(ie_ref, it_ref, lo_ref, hi_ref, first_ref, n_ref, xs_ref, wg_ref, wu_ref, wd_ref, ys_ref,
                wg_b, wu_b, wd_b):
    w = pl.program_id(0)

    @pl.when(w < n_ref[0])
    def _():
        @pl.when((w == 0) | (ie_ref[w] != ie_ref[jnp.maximum(w - 1, 0)]))
        def _():
            wg_b[...] = wg_ref[...].astype(BF16)
            wu_b[...] = wu_ref[...].astype(BF16)
            wd_b[...] = wd_ref[...].astype(BF16)

        half = wg_b.shape[0] // 2
        s = half // LANES
        x_lo, x_hi = _unpack_pair(_load_token_tiles(xs_ref, s))
        x_lo, x_hi = x_lo.astype(BF16), x_hi.astype(BF16)
        hg = _dot(x_lo, wg_b[0:half, :]) + _dot(x_hi, wg_b[half:, :])
        hu = _dot(x_lo, wu_b[0:half, :]) + _dot(x_hi, wu_b[half:, :])
        yp = _pack_row(_dot((_silu(hg) * hu).astype(BF16), wd_b[...]))
        rows = lax.broadcasted_iota(I32, yp.shape, 0)
        mine = (rows >= lo_ref[w]) & (rows < hi_ref[w])

        @pl.when(first_ref[w] == 1)
        def _():
            _store_token_tiles(ys_ref, jnp.where(mine, yp, jnp.uint32(0)))

        @pl.when(first_ref[w] == 0)
        def _():
            _store_token_tiles(ys_ref, jnp.where(mine, yp, _load_token_tiles(ys_ref, s)))


def _moe_experts(xs, plan, layer, wg, wu, wd):
    _, _, d, de = wg.shape
    s = d // 2 // LANES
    tg = GROUP_TILE
    n_items_max = plan[0].shape[0]
    grid_spec = pltpu.PrefetchScalarGridSpec(
        num_scalar_prefetch=6,
        grid=(n_items_max,),
        in_specs=[
            pl.BlockSpec((tg * s, LANES), lambda w, ie, it, lo, hi, fi, n: (it[w], 0)),
            pl.BlockSpec((None, None, d, de), lambda w, ie, it, lo, hi, fi, n: (layer, ie[w], 0, 0)),
            pl.BlockSpec((None, None, d, de), lambda w, ie, it, lo, hi, fi, n: (layer, ie[w], 0, 0)),
            pl.BlockSpec((None, None, de, d), lambda w, ie, it, lo, hi, fi, n: (layer, ie[w], 0, 0)),
        ],
        out_specs=pl.BlockSpec((tg * s, LANES), lambda w, ie, it, lo, hi, fi, n: (it[w], 0)),
        scratch_shapes=[pltpu.VMEM((d, de), BF16), pltpu.VMEM((d, de), BF16), pltpu.VMEM((de, d), BF16)],
    )
    return pl.pallas_call(
        _gmm_kernel,
        grid_spec=grid_spec,
        out_shape=jax.ShapeDtypeStruct(xs.shape, U32),
        compiler_params=_params("arbitrary"),
        name="moe_experts",
    )(*plan, xs, wg, wu, wd)


def _moe_combine(y0_ref, y1_ref, wts_ref, s):
    wts = wts_ref[...]
    return (wts[:, 0:1] * _unpack_row(_load_token_tiles(y0_ref, s))
            + wts[:, 1:2] * _unpack_row(_load_token_tiles(y1_ref, s)))


def _inproj_kernel(x_ref, y0_ref, y1_ref, wts_ref, mp_ref, m_ref, nw_ref, w_ref, wdt_ref, cw_ref, cb_ref, dtb_ref,
                   lat_ref, z_ref, xbc_ref, dt_ref, h_ref, mask_ref, *, nz, n_lat_tiles, ctx_row):
    i = pl.program_id(0)
    j = pl.program_id(1)
    n_sub = w_ref.shape[1] // COL_TILE

    @pl.when(j == 0)
    def _():
        s = x_ref.shape[1] // 2 // LANES
        lat = x_ref[...] + mp_ref[5:6, :] * _moe_combine(y0_ref, y1_ref, wts_ref, s)
        lat_ref[...] = lat
        h_ref[...] = _mod_norm(lat, nw_ref[...], m_ref[0:1, :], m_ref[1:2, :]).astype(BF16)
        _conv_masks(mask_ref, jnp.where(i < n_lat_tiles, GRID_W, ctx_row))

    @pl.when(j < nz)
    def _():
        for s in range(n_sub):
            sl = slice(s * COL_TILE, (s + 1) * COL_TILE)
            z_ref[:, sl] = _dot(h_ref[...], w_ref[:, sl]).astype(BF16)

    @pl.when(j >= nz)
    def _():
        for s in range(n_sub):
            sl = slice(s * COL_TILE, (s + 1) * COL_TILE)
            r = _dot(h_ref[...], w_ref[:, sl])
            xbc_ref[:, sl] = _silu(_dwconv_rows(r, cw_ref, mask_ref, s * COL_TILE) + cb_ref[:, sl]).astype(BF16)

    @pl.when(j == pl.num_programs(1) - 1)
    def _():
        v = _dot(h_ref[...], wdt_ref[...]) + dtb_ref[...]
        dt_ref[...] = jnp.maximum(v, 0.0) + jnp.log1p(jnp.exp(-jnp.abs(v)))


def _ssd_in_proj(tokens, y2, wts, mods, layer, nw, w_zx, w_dt, conv_w, conv_b, dt_bias,
                 *, seq, n_lat_tiles, ctx_row, d_inner, conv_dim):
    t, d = tokens.shape
    tm = TOKEN_TILE
    tc = next(c for c in (4 * COL_TILE, 2 * COL_TILE, COL_TILE) if d_inner % c == 0 and conv_dim % c == 0)
    nz, nx = d_inner // tc, conv_dim // tc
    mod_row = _mod_row_fn(mods, seq, tm)

    def xcol(j):
        return jnp.maximum(j - nz, 0)

    kern = functools.partial(_inproj_kernel, nz=nz, n_lat_tiles=n_lat_tiles, ctx_row=ctx_row)
    return pl.pallas_call(
        kern,
        grid=(t // tm, nz + nx),
        in_specs=[
            pl.BlockSpec((tm, d), lambda i, j: (i, 0)),
            pl.BlockSpec((None, tm * (d // 2 // LANES), LANES), lambda i, j: (0, i, 0)),
            pl.BlockSpec((None, tm * (d // 2 // LANES), LANES), lambda i, j: (1, i, 0)),
            pl.BlockSpec((tm, LANES), lambda i, j: (i, 0)),
            pl.BlockSpec((None, None, 6, d), lambda i, j: (layer - 1, mod_row(i), 0, 0)),
            pl.BlockSpec((None, None, 6, d), lambda i, j: (layer, mod_row(i), 0, 0)),
            pl.BlockSpec((1, d), lambda i, j: (0, 0)),
            pl.BlockSpec((d, tc), lambda i, j: (0, j)),
            pl.BlockSpec((d, LANES), lambda i, j: (0, 0)),
            pl.BlockSpec((conv_w.shape[0], tc), lambda i, j: (0, xcol(j))),
            pl.BlockSpec((1, tc), lambda i, j: (0, xcol(j))),
            pl.BlockSpec((1, LANES), lambda i, j: (0, 0)),
        ],
        out_specs=[
            pl.BlockSpec((tm, d), lambda i, j: (i, 0)),
            pl.BlockSpec((tm, tc), lambda i, j: (i, jnp.minimum(j, nz - 1))),
            pl.BlockSpec((tm, tc), lambda i, j: (i, xcol(j))),
            pl.BlockSpec((tm, LANES), lambda i, j: (i, 0)),
        ],
        out_shape=[
            jax.ShapeDtypeStruct((t, d), F32),
            jax.ShapeDtypeStruct((t, d_inner), BF16),
            jax.ShapeDtypeStruct((t, conv_dim), BF16),
            jax.ShapeDtypeStruct((t, LANES), F32),
        ],
        scratch_shapes=[pltpu.VMEM((tm, d), BF16), pltpu.VMEM((conv_w.shape[0] - 1, tm, LANES), F32)],
        compiler_params=_params("arbitrary", "arbitrary"),
        name="ssd_in_proj",
    )(tokens, y2, y2, wts, mods, mods, nw, w_zx, w_dt, conv_w, conv_b, dt_bias)


def _ssd_kernel(x_ref, b_ref, c_ref, dt_ref, a_ref, y_ref, st_ref, *, n_heads):
    d = pl.program_id(1)
    k = pl.program_id(2)
    q = x_ref.shape[0]
    n_groups = st_ref.shape[0]
    gw = st_ref.shape[2]
    hpg = gw // HEAD_DIM
    pw = 2 * HEAD_DIM
    fwd = d == 0

    @pl.when(k == 0)
    def _():
        st_ref[...] = jnp.zeros_like(st_ref)

    t_i = lax.broadcasted_iota(I32, (q, q), 0)
    s_i = lax.broadcasted_iota(I32, (q, q), 1)
    incl = jnp.where(fwd, t_i - s_i, s_i - t_i) >= 0

    dt_all = dt_ref[...]
    dta_all = dt_all * (a_ref[...] * LOG2_E)
    cum_all = jnp.dot(jnp.where(incl, 1.0, 0.0), dta_all, preferred_element_type=F32,
                      precision=lax.Precision.HIGHEST)
    cum_t_all = cum_all.T
    dt_t_all = dt_all.T
    tot_t_all = jnp.sum(dta_all.T, axis=1, keepdims=True)

    def pick_cols(a):
        return jnp.where(fwd, a[:, 0:n_heads], a[:, n_heads:2 * n_heads])

    def pick_rows(a):
        return jnp.where(fwd, a[0:n_heads, :], a[n_heads:2 * n_heads, :])

    cum = pick_cols(cum_all)
    tot = pick_cols(jnp.sum(dta_all, axis=0, keepdims=True))
    cum_t = pick_rows(cum_t_all)
    dt_t = pick_rows(dt_t_all)
    to_end_t = (dt_t * jnp.exp2(pick_rows(tot_t_all) - cum_t)).astype(BF16)
    dt_tb = dt_t.astype(BF16)
    dec_all = jnp.exp2(tot)

    low_half = lax.broadcasted_iota(I32, (q, pw), 1) < HEAD_DIM
    low_row = lax.broadcasted_iota(I32, (1, pw), 1) < HEAD_DIM

    for g in range(n_groups):
        bg = b_ref[:, g * D_STATE:(g + 1) * D_STATE]
        cg = c_ref[:, g * D_STATE:(g + 1) * D_STATE]
        cb = lax.dot_general(cg, bg, (((1,), (1,)), ((), ())), preferred_element_type=F32)
        cb = jnp.where(incl, cb, 0.0).astype(BF16)
        bg_t = bg.astype(F32).T.astype(BF16)
        st = st_ref[g]
        y_off = _dot(cg, st.astype(BF16))

        for pair in range(hpg // 2):
            psl = slice(pair * pw, (pair + 1) * pw)
            xp = x_ref[:, g * gw + pair * pw:g * gw + (pair + 1) * pw]
            zero = jnp.zeros_like(xp)
            rhs = jnp.concatenate([jnp.where(low_half, xp, zero), jnp.where(low_half, zero, xp)], axis=0)
            w_parts, u_parts, cols = [], [], []
            for hh in range(2):
                h = g * hpg + pair * 2 + hh
                col = jnp.broadcast_to(cum[:, h:h + 1], (q, q))
                seg = jnp.exp2(jnp.minimum(col - cum_t[h:h + 1, :], 0.0))
                w_parts.append(cb * seg.astype(BF16) * dt_tb[h:h + 1, :])
                u_parts.append(bg_t * to_end_t[h:h + 1, :])
                cols.append(col)
            y_diag = _dot(jnp.concatenate(w_parts, axis=1), rhs)
            upd = _dot(jnp.concatenate(u_parts, axis=1), rhs)
            dec_in = jnp.exp2(jnp.where(low_half, cols[0], cols[1]))
            y_ref[:, g * gw + pair * pw:g * gw + (pair + 1) * pw] = (y_diag + y_off[:, psl] * dec_in).astype(BF16)
            h0 = g * hpg + pair * 2
            dec_pair = jnp.where(low_row, jnp.broadcast_to(dec_all[:, h0:h0 + 1], (1, pw)),
                                 jnp.broadcast_to(dec_all[:, h0 + 1:h0 + 2], (1, pw)))
            st_ref[g, :, psl] = st[:, psl] * dec_pair + upd


def _ssd_scan(xbc, dt, a_neg, *, n_batch, lat_chunks, ctx_chunks, d_inner, n_heads):
    t = xbc.shape[0]
    q = CHUNK
    gn = SSD_GROUPS * D_STATE
    gw = d_inner // SSD_GROUPS
    assert gw % (2 * HEAD_DIM) == 0 and d_inner % gn == 0 and q == 2 * HEAD_DIM and q == LANES
    n_steps = ctx_chunks + lat_chunks
    ctx_base = n_batch * lat_chunks

    def chunk(b, d, k):
        kc = jnp.where(d == 0, k, ctx_chunks - 1 - k)
        kl = jnp.where(d == 0, k - ctx_chunks, n_steps - 1 - k)
        return jnp.where(k < ctx_chunks, ctx_base + b * ctx_chunks + kc, b * lat_chunks + kl)

    kern = functools.partial(_ssd_kernel, n_heads=n_heads)
    return pl.pallas_call(
        kern,
        grid=(n_batch, 2, n_steps),
        in_specs=[
            pl.BlockSpec((q, d_inner), lambda b, d, k: (chunk(b, d, k), 0)),
            pl.BlockSpec((q, gn), lambda b, d, k: (chunk(b, d, k), d_inner // gn)),
            pl.BlockSpec((q, gn), lambda b, d, k: (chunk(b, d, k), d_inner // gn + 1)),
            pl.BlockSpec((q, LANES), lambda b, d, k: (chunk(b, d, k), 0)),
            pl.BlockSpec((1, LANES), lambda b, d, k: (0, 0)),
        ],
        out_specs=pl.BlockSpec((None, q, d_inner), lambda b, d, k: (d, chunk(b, d, k), 0)),
        out_shape=jax.ShapeDtypeStruct((2, t, d_inner), BF16),
        scratch_shapes=[pltpu.VMEM((SSD_GROUPS, D_STATE, gw), F32)],
        compiler_params=_params("arbitrary", "arbitrary", "arbitrary"),
        name="ssd_scan",
    )(xbc, xbc, xbc, dt, a_neg)


def _outproj_kernel(yf_ref, yb_ref, xs_ref, z_ref, dsk_ref, gnw_ref, lat_ref, m_ref, w_ref, o_ref):
    gw = w_ref.shape[0] // SSD_GROUPS
    acc = None
    for g in range(SSD_GROUPS):
        sl = slice(g * gw, (g + 1) * gw)
        y = (yf_ref[:, sl].astype(F32) + yb_ref[:, sl].astype(F32)
             + dsk_ref[:, sl] * xs_ref[:, sl].astype(F32))
        gated = y * _silu(z_ref[:, sl].astype(F32))
        ms = jnp.mean(gated * gated, axis=-1, keepdims=True)
        gn = (gated * lax.rsqrt(ms + EPS) * gnw_ref[:, sl]).astype(BF16)
        part = _dot(gn, w_ref[sl, :])
        acc = part if acc is None else acc + part
    o_ref[...] = lat_ref[...] + m_ref[2:3, :] * acc


def _ssd_out_proj(y, xbc, z, d_skip, gnw, tokens, mods, layer, w_out, *, n_tok, seq):
    d_inner, d = w_out.shape
    tm = OUT_TOKEN_TILE
    mod_row = _mod_row_fn(mods, seq, tm)
    return pl.pallas_call(
        _outproj_kernel,
        grid=(n_tok // tm,),
        in_specs=[
            pl.BlockSpec((None, tm, d_inner), lambda i: (0, i, 0)),
            pl.BlockSpec((None, tm, d_inner), lambda i: (1, i, 0)),
            pl.BlockSpec((tm, d_inner), lambda i: (i, 0)),
            pl.BlockSpec((tm, d_inner), lambda i: (i, 0)),
            pl.BlockSpec((1, d_inner), lambda i: (0, 0)),
            pl.BlockSpec((1, d_inner), lambda i: (0, 0)),
            pl.BlockSpec((tm, d), lambda i: (i, 0)),
            pl.BlockSpec((None, None, 6, d), lambda i: (layer, mod_row(i), 0, 0)),
            pl.BlockSpec((d_inner, d), lambda i: (0, 0), pipeline_mode=pl.Buffered(1)),
        ],
        out_specs=pl.BlockSpec((tm, d), lambda i: (i, 0)),
        out_shape=jax.ShapeDtypeStruct((n_tok, d), F32),
        compiler_params=_params("arbitrary"),
        name="ssd_out_proj",
    )(y, y, xbc, z, d_skip, gnw, tokens, mods, w_out)


def _final_kernel(x_ref, y0_ref, y1_ref, wts_ref, m_ref, nw_ref, o_ref):
    s = x_ref.shape[1] // 2 // LANES
    lat = x_ref[...] + m_ref[5:6, :] * _moe_combine(y0_ref, y1_ref, wts_ref, s)
    ms = jnp.mean(lat * lat, axis=-1, keepdims=True)
    o_ref[...] = lat * lax.rsqrt(ms + EPS) * nw_ref[...]


def _final_norm(tokens, y2, wts, mods, layer, nw, *, seq):
    t, d = tokens.shape
    tm = TOKEN_TILE
    mod_row = _mod_row_fn(mods, seq, tm)
    return pl.pallas_call(
        _final_kernel,
        grid=(t // tm,),
        in_specs=[
            pl.BlockSpec((tm, d), lambda i: (i, 0)),
            pl.BlockSpec((None, tm * (d // 2 // LANES), LANES), lambda i: (0, i, 0)),
            pl.BlockSpec((None, tm * (d // 2 // LANES), LANES), lambda i: (1, i, 0)),
            pl.BlockSpec((tm, LANES), lambda i: (i, 0)),
            pl.BlockSpec((None, None, 6, d), lambda i: (layer, mod_row(i), 0, 0)),
            pl.BlockSpec((1, d), lambda i: (0, 0)),
        ],
        out_specs=pl.BlockSpec((tm, d), lambda i: (i, 0)),
        out_shape=jax.ShapeDtypeStruct((t, d), F32),
        compiler_params=_params("arbitrary"),
        name="final_norm",
    )(tokens, y2, y2, wts, mods, nw)


def _router_weights(rg_w, rg_b, re_w, re_b):
    d, n_groups = rg_w.shape
    n_exp = re_w.shape[1]
    pad = LANES - n_groups - n_exp
    wr = jnp.concatenate([rg_w, re_w, jnp.zeros((d, pad), F32)], axis=1).astype(BF16)
    br = jnp.concatenate([rg_b, re_b, jnp.zeros((pad,), F32)]).reshape(1, LANES)
    return wr, br


def _moe(tokens, n_tok, mods, layer, nw, rg_w, rg_b, re_w, re_b, w_gate, w_up, w_down, *, seq):
    n_groups = rg_w.shape[1]
    n_exp = re_w.shape[1]
    s = tokens.shape[1] // 2 // LANES
    assert n_tok % PERM_TILE == 0 and (TOP_K * n_tok) % GROUP_TILE == 0 and tokens.shape[1] % (2 * LANES) == 0
    wr, br = _router_weights(rg_w, rg_b, re_w, re_b)
    fp, meta, wts, counts = _moe_route(tokens, n_tok, mods, layer, nw, wr, br, seq=seq, n_groups=n_groups,
                                       per_group=n_exp // n_groups)
    pos_blocks, plan = _route_plan(meta, counts, n_tok=n_tok, n_exp=n_exp, n_groups=n_groups,
                                   tg=GROUP_TILE, tp=PERM_TILE)
    xs = _sort_rows(pos_blocks, fp, n_tok=n_tok, s=s)
    ys = _moe_experts(xs, plan, layer, w_gate, w_up, w_down)
    return _unsort_rows(pos_blocks, ys, n_tok=n_tok, s=s), wts


def kernel(x, c, ctx, c_ctx, ada_w, ada_b, norm_mix_w, norm_ffn_w, sc_w_in, sc_conv_w, sc_w_out, ssd_w_in,
           ssd_conv_w, ssd_conv_b, ssd_dt_bias, ssd_a_log, ssd_d, ssd_norm_w, ssd_w_out, rg_w, rg_b, re_w, re_b,
           moe_w_gate, moe_w_up, moe_w_down, final_norm_w):
    n_batch, seq, d = x.shape
    ctx_len = ctx.shape[1]
    depth = ada_w.shape[0]
    d_inner = ssd_w_out.shape[1]
    conv_dim = ssd_conv_w.shape[2]
    n_heads = ssd_d.shape[1]
    tm = TOKEN_TILE
    t_lat, t_ctx = n_batch * seq, n_batch * ctx_len
    assert depth == 2 and seq % tm == 0 and t_ctx % tm == 0 and tm % ctx_len == 0 and tm % GRID_W == 0
    assert ctx_len & (ctx_len - 1) == 0 and GRID_W & (GRID_W - 1) == 0
    assert seq % CHUNK == 0 and ctx_len % CHUNK == 0 and d_inner == n_heads * HEAD_DIM
    assert n_batch + 1 <= SUBLANES and 2 * n_heads <= LANES and d_inner % COL_TILE == 0 and conv_dim % COL_TILE == 0
    n_lat_tiles = t_lat // tm

    cvec = jnp.zeros((SUBLANES, d), F32).at[0:n_batch].set(c).at[n_batch].set(c_ctx)
    mods = _ada_mods(cvec, ada_w, ada_b)

    x_lat = x.reshape(t_lat, d)
    x_ctx = ctx.reshape(t_ctx, d)

    tokens = _short_conv_layer(x_lat, x_ctx, mods, 0, norm_mix_w[0:1], sc_w_in[0].astype(BF16), sc_conv_w[0],
                               sc_w_out[0].astype(BF16), seq=seq, ctx_row=ctx_len)
    t_all = t_lat + t_ctx
    y2, wts = _moe(tokens, t_all, mods, 0, norm_ffn_w[0:1], rg_w[0], rg_b[0], re_w[0], re_b[0],
                   moe_w_gate, moe_w_up, moe_w_down, seq=seq)

    w_in = ssd_w_in[0]
    dt_cols = 2 * n_heads
    w_zx = w_in[:, :d_inner + conv_dim].astype(BF16)
    w_dt = jnp.concatenate([w_in[:, d_inner + conv_dim:], jnp.zeros((d, LANES - dt_cols), F32)],
                           axis=1).astype(BF16)
    dt_bias = jnp.zeros((1, LANES), F32).at[0, :dt_cols].set(ssd_dt_bias[0].reshape(-1))
    a_neg = jnp.zeros((1, LANES), F32).at[0, :dt_cols].set(-jnp.exp(ssd_a_log[0].reshape(-1)))
    tokens, z, xbc, dt = _ssd_in_proj(tokens, y2, wts, mods, 1, norm_mix_w[1:2], w_zx, w_dt, ssd_conv_w[0],
                                      ssd_conv_b[0].reshape(1, conv_dim), dt_bias, seq=seq,
                                      n_lat_tiles=n_lat_tiles, ctx_row=ctx_len, d_inner=d_inner, conv_dim=conv_dim)
    y = _ssd_scan(xbc, dt, a_neg, n_batch=n_batch, lat_chunks=seq // CHUNK, ctx_chunks=ctx_len // CHUNK,
                  d_inner=d_inner, n_heads=n_heads)
    d_skip = jnp.repeat(ssd_d[0], HEAD_DIM).reshape(1, d_inner)
    lat = _ssd_out_proj(y, xbc, z, d_skip, ssd_norm_w[0].reshape(1, d_inner), tokens, mods, 1,
                        ssd_w_out[0].astype(BF16), n_tok=t_lat, seq=seq)

    y2, wts = _moe(lat, t_lat, mods, 1, norm_ffn_w[1:2], rg_w[1], rg_b[1], re_w[1], re_b[1],
                   moe_w_gate, moe_w_up, moe_w_down, seq=seq)
    out = _final_norm(lat, y2, wts, mods, 1, final_norm_w.reshape(1, d), seq=seq)
    return out.reshape(n_batch, seq, d)
```

```python
import functools

import jax
import jax.numpy as jnp
from jax import lax
from jax.experimental import pallas as pl
from jax.experimental.pallas import tpu as pltpu

F32 = jnp.float32
BF16 = jnp.bfloat16
I32 = jnp.int32
U32 = jnp.uint32

EPS = 1e-6
GRID_W = 64
HEAD_DIM = 64
D_STATE = 128
SSD_GROUPS = 8
CHUNK = 128
TOP_K = 2

LANES = 128
SUBLANES = 8
VMEM_LIMIT = 58 * 1024 * 1024
TOKEN_TILE = 512
COL_TILE = 512
OUT_TOKEN_TILE = 256
GROUP_TILE = 256
PERM_TILE = 512
ADA_COL_TILE = 1024
HI_MASK = 0xFFFF0000
LOG2_E = 1.4426950408889634


def _params(*sem):
    return pltpu.CompilerParams(dimension_semantics=sem, vmem_limit_bytes=VMEM_LIMIT)


def _dot(a, b):
    return jnp.dot(a, b, preferred_element_type=F32)


def _silu(x):
    return x * jax.nn.sigmoid(x)


def _mod_norm(x, nw, shift, scale):
    ms = jnp.mean(x * x, axis=-1, keepdims=True)
    y = x * lax.rsqrt(ms + EPS)
    return (y * nw) * (1.0 + scale) + shift


def _pack_pair(lo, hi):
    lo_bits = lax.bitcast_convert_type(lo.astype(BF16).astype(F32), U32)
    hi_bits = lax.bitcast_convert_type(hi.astype(BF16).astype(F32), U32)
    return (lo_bits >> 16) | (hi_bits & jnp.uint32(HI_MASK))


def _unpack_pair(u):
    lo = lax.bitcast_convert_type(u << 16, F32)
    hi = lax.bitcast_convert_type(u & jnp.uint32(HI_MASK), F32)
    return lo, hi


def _pack_row(x):
    half = x.shape[1] // 2
    return _pack_pair(x[:, :half], x[:, half:])


def _unpack_row(u):
    lo, hi = _unpack_pair(u)
    return jnp.concatenate([lo, hi], axis=1)


def _store_token_tiles(ref, packed):
    rows, width = packed.shape
    s = width // LANES
    for c in range(s):
        ref[pl.ds(c, rows, stride=s), :] = packed[:, c * LANES:(c + 1) * LANES]


def _load_token_tiles(ref, s):
    rows = ref.shape[0] // s
    return jnp.concatenate([ref[pl.ds(c, rows, stride=s), :] for c in range(s)], axis=1)


def _conv_masks(mask_ref, row_len):
    n_side, tm, _ = mask_ref.shape
    half = n_side // 2
    pos = lax.broadcasted_iota(I32, (tm, LANES), 0) & (row_len - 1)
    for m in range(n_side):
        off = m - half if m < half else m - half + 1
        valid = (pos + off >= 0) & (pos + off < row_len)
        mask_ref[m] = jnp.where(valid, 1.0, 0.0)


def _dwconv_rows(u, cw_ref, mask_ref, col0):
    tm, width_c = u.shape
    width = cw_ref.shape[0]
    half = width // 2
    outs = []
    for c in range(width_c // LANES):
        sl = slice(c * LANES, (c + 1) * LANES)
        wsl = slice(col0 + c * LANES, col0 + (c + 1) * LANES)
        uc = u[:, sl]
        out = cw_ref[half:half + 1, wsl] * uc
        for k in range(width):
            off = k - half
            if off == 0:
                continue
            m = k if k < half else k - 1
            shifted = pltpu.roll(uc, (-off) % tm, axis=0)
            out = out + cw_ref[k:k + 1, wsl] * (shifted * mask_ref[m])
        outs.append(out)
    return outs[0] if len(outs) == 1 else jnp.concatenate(outs, axis=1)


def _ada_kernel(c_ref, w_ref, b_ref, o_ref):
    s = _silu(c_ref[...])
    o_ref[...] = jnp.dot(s, w_ref[...], preferred_element_type=F32,
                         precision=lax.Precision.HIGHEST) + b_ref[...]


def _ada_mods(cvec, ada_w, ada_b):
    depth, d, d6 = ada_w.shape
    tn = min(ADA_COL_TILE, d6)
    out = pl.pallas_call(
        _ada_kernel,
        grid=(depth, d6 // tn),
        in_specs=[pl.BlockSpec((SUBLANES, d), lambda l, j: (0, 0)),
                  pl.BlockSpec((None, d, tn), lambda l, j: (l, 0, j)),
                  pl.BlockSpec((None, 1, tn), lambda l, j: (l, 0, j))],
        out_specs=pl.BlockSpec((None, SUBLANES, tn), lambda l, j: (l, 0, j)),
        out_shape=jax.ShapeDtypeStruct((depth, SUBLANES, d6), F32),
        compiler_params=_params("arbitrary", "arbitrary"),
        name="ada_mods",
    )(cvec, ada_w, ada_b.reshape(depth, 1, d6))
    return out.reshape(depth, SUBLANES, 6, d)


def _mod_row_fn(mods, tokens_per_batch, tile):
    n_rows = mods.shape[1]
    per_batch = tokens_per_batch // tile

    def mod_row(i):
        return jnp.minimum(i // per_batch, n_rows - 1)
    return mod_row


def _sc_kernel(xl_ref, xc_ref, m_ref, nw_ref, wb_ref, wc_ref, wv_ref, cw_ref, wo_ref,
               o_ref, h_ref, acc_ref, mask_ref, *, n_lat_tiles, ctx_row):
    i = pl.program_id(0)
    j = pl.program_id(1)
    is_lat = i < n_lat_tiles

    def for_source(fn):
        pl.when(is_lat)(lambda: fn(xl_ref))
        pl.when(jnp.logical_not(is_lat))(lambda: fn(xc_ref))

    @pl.when(j == 0)
    def _():
        def prologue(x_ref):
            h_ref[...] = _mod_norm(x_ref[...], nw_ref[...], m_ref[0:1, :], m_ref[1:2, :]).astype(BF16)
        for_source(prologue)
        acc_ref[...] = jnp.zeros_like(acc_ref)
        _conv_masks(mask_ref, jnp.where(is_lat, GRID_W, ctx_row))

    h = h_ref[...]
    gate_b = _dot(h, wb_ref[...])
    u = _dot(h, wc_ref[...]) * _dot(h, wv_ref[...])
    g = (gate_b * _dwconv_rows(u, cw_ref, mask_ref, 0)).astype(BF16)
    acc_ref[...] += _dot(g, wo_ref[...])

    @pl.when(j == pl.num_programs(1) - 1)
    def _():
        def epilogue(x_ref):
            o_ref[...] = x_ref[...] + m_ref[2:3, :] * acc_ref[...]
        for_source(epilogue)


def _short_conv_layer(x_lat, x_ctx, mods, layer, nw, w_in, conv_w, w_out, *, seq, ctx_row):
    t_lat, d = x_lat.shape
    t_ctx = x_ctx.shape[0]
    tm, tc = TOKEN_TILE, min(COL_TILE, d)
    n_lat, n_ctx, nj = t_lat // tm, t_ctx // tm, d // tc
    mod_row = _mod_row_fn(mods, seq, tm)
    kern = functools.partial(_sc_kernel, n_lat_tiles=n_lat, ctx_row=ctx_row)
    return pl.pallas_call(
        kern,
        grid=(n_lat + n_ctx, nj),
        in_specs=[
            pl.BlockSpec((tm, d), lambda i, j: (jnp.minimum(i, n_lat - 1), 0)),
            pl.BlockSpec((tm, d), lambda i, j: (jnp.maximum(i - n_lat, 0), 0)),
            pl.BlockSpec((None, None, 6, d), lambda i, j: (layer, mod_row(i), 0, 0)),
            pl.BlockSpec((1, d), lambda i, j: (0, 0)),
            pl.BlockSpec((d, tc), lambda i, j: (0, j)),
            pl.BlockSpec((d, tc), lambda i, j: (0, nj + j)),
            pl.BlockSpec((d, tc), lambda i, j: (0, 2 * nj + j)),
            pl.BlockSpec((conv_w.shape[0], tc), lambda i, j: (0, j)),
            pl.BlockSpec((tc, d), lambda i, j: (j, 0)),
        ],
        out_specs=pl.BlockSpec((tm, d), lambda i, j: (i, 0)),
        out_shape=jax.ShapeDtypeStruct((t_lat + t_ctx, d), F32),
        scratch_shapes=[pltpu.VMEM((tm, d), BF16), pltpu.VMEM((tm, d), F32),
                        pltpu.VMEM((conv_w.shape[0] - 1, tm, LANES), F32)],
        compiler_params=_params("arbitrary", "arbitrary"),
        name="short_conv_mixer",
    )(x_lat, x_ctx, mods, nw, w_in, w_in, w_in, conv_w, w_out)


def _route_kernel(x_ref, m_ref, nw_ref, wr_ref, br_ref, fp_ref, meta_ref, wts_ref, cnt_ref, run_ref,
                  *, n_groups, per_group):
    i = pl.program_id(0)

    @pl.when(i == 0)
    def _():
        run_ref[...] = jnp.zeros_like(run_ref)

    f = _mod_norm(x_ref[...], nw_ref[...], m_ref[3:4, :], m_ref[4:5, :])
    _store_token_tiles(fp_ref, _pack_row(f))
    logits = _dot(f.astype(BF16), wr_ref[...]) + br_ref[...]
    tm = logits.shape[0]
    lane = lax.broadcasted_iota(I32, logits.shape, 1).astype(F32)
    neg = -jnp.inf

    g_mask = lane < n_groups
    gl = jnp.where(g_mask, logits, neg)
    g_max = jnp.max(gl, axis=1, keepdims=True)
    g_sel = jnp.min(jnp.where(gl == g_max, lane, LANES), axis=1, keepdims=True)
    p_group = 1.0 / jnp.sum(jnp.where(g_mask, jnp.exp(gl - g_max), 0.0), axis=1, keepdims=True)

    e_lo = n_groups + g_sel * per_group
    in_group = (lane >= e_lo) & (lane < e_lo + per_group)
    el = jnp.where(in_group, logits, neg)
    e_max = jnp.max(el, axis=1, keepdims=True)
    ex = jnp.where(in_group, jnp.exp(el - e_max), 0.0)
    p = jnp.where(in_group, ex / jnp.sum(ex, axis=1, keepdims=True), -1.0)
    p1 = jnp.max(p, axis=1, keepdims=True)
    l1 = jnp.min(jnp.where(p == p1, lane, LANES), axis=1, keepdims=True)
    p_rest = jnp.where(lane == l1, -1.0, p)
    p2 = jnp.max(p_rest, axis=1, keepdims=True)
    l2 = jnp.min(jnp.where(p_rest == p2, lane, LANES), axis=1, keepdims=True)
    denom = p1 + p2
    w1 = p1 / denom * p_group
    w2 = p2 / denom * p_group

    hit1 = lane == l1
    hit2 = lane == l2
    onehot = jnp.where(hit1 | hit2, 1.0, 0.0)
    earlier = lax.broadcasted_iota(I32, (tm, tm), 0) > lax.broadcasted_iota(I32, (tm, tm), 1)
    before = _dot(jnp.where(earlier, 1.0, 0.0).astype(BF16), onehot.astype(BF16)) + run_ref[...]
    r1 = jnp.sum(jnp.where(hit1, before, 0.0), axis=1, keepdims=True)
    r2 = jnp.sum(jnp.where(hit2, before, 0.0), axis=1, keepdims=True)
    run_ref[...] += jnp.sum(onehot, axis=0, keepdims=True)
    cnt_ref[...] = run_ref[...].astype(I32)

    meta = jnp.where(lane == 0, l1 - n_groups,
                     jnp.where(lane == 1, l2 - n_groups, jnp.where(lane == 2, r1, r2)))
    meta_ref[...] = meta.astype(I32)
    wts_ref[...] = jnp.where(lane == 0, w1, w2)


def _moe_route(tokens, n_tok, mods, layer, nw, wr, br, *, seq, n_groups, per_group):
    d = tokens.shape[1]
    tm = TOKEN_TILE
    s = d // 2 // LANES
    mod_row = _mod_row_fn(mods, seq, tm)
    kern = functools.partial(_route_kernel, n_groups=n_groups, per_group=per_group)
    return pl.pallas_call(
        kern,
        grid=(n_tok // tm,),
        in_specs=[
            pl.BlockSpec((tm, d), lambda i: (i, 0)),
            pl.BlockSpec((None, None, 6, d), lambda i: (layer, mod_row(i), 0, 0)),
            pl.BlockSpec((1, d), lambda i: (0, 0)),
            pl.BlockSpec((d, LANES), lambda i: (0, 0)),
            pl.BlockSpec((1, LANES), lambda i: (0, 0)),
        ],
        out_specs=[
            pl.BlockSpec((tm * s, LANES), lambda i: (i, 0)),
            pl.BlockSpec((tm, LANES), lambda i: (i, 0)),
            pl.BlockSpec((tm, LANES), lambda i: (i, 0)),
            pl.BlockSpec((1, LANES), lambda i: (0, 0)),
        ],
        out_shape=[
            jax.ShapeDtypeStruct((n_tok * s, LANES), U32),
            jax.ShapeDtypeStruct((n_tok, LANES), I32),
            jax.ShapeDtypeStruct((n_tok, LANES), F32),
            jax.ShapeDtypeStruct((1, LANES), I32),
        ],
        scratch_shapes=[pltpu.VMEM((1, LANES), F32)],
        compiler_params=_params("arbitrary"),
        name="moe_route",
    )(tokens, mods, nw, wr, br)


def _route_plan(meta, counts, *, n_tok, n_exp, n_groups, tg, tp):
    ids = meta[:, 0:TOP_K]
    rank = meta[:, TOP_K:2 * TOP_K]
    cnt = counts[0, n_groups:n_groups + n_exp]
    ends = jnp.cumsum(cnt)
    offs = ends - cnt
    pos = offs[ids] + rank
    pos_blocks = pos.T.reshape(TOP_K, n_tok // tp, tp).transpose(1, 0, 2)
    n_row_tiles = TOP_K * n_tok // tg
    first_tile = offs // tg
    n_it = jnp.where(cnt > 0, (ends - 1) // tg - first_tile + 1, 0)
    it_end = jnp.cumsum(n_it)
    it_start = it_end - n_it
    n_items = it_end[-1]
    w = jnp.minimum(jnp.arange(n_row_tiles + n_exp - 1, dtype=I32), n_items - 1)
    item_exp = jnp.sum(it_end[None, :] <= w[:, None], axis=1).astype(I32)
    item_tile = (first_tile[item_exp] + w - it_start[item_exp]).astype(I32)
    row_lo = jnp.maximum(offs[item_exp] - item_tile * tg, 0).astype(I32)
    row_hi = jnp.minimum(ends[item_exp] - item_tile * tg, tg).astype(I32)
    first = jnp.concatenate([jnp.ones((1,), I32), (item_tile[1:] != item_tile[:-1]).astype(I32)])
    return pos_blocks.astype(I32), (item_exp, item_tile, row_lo, row_hi, first, n_items.astype(I32).reshape(1))


PERM_SLOTS = 3


def _token_copy(src, src_tok, dst, dst_tok, s, sem):
    return pltpu.make_async_copy(src.at[pl.ds(pl.multiple_of(src_tok * s, s), s), :],
                                 dst.at[pl.ds(pl.multiple_of(dst_tok * s, s), s), :], sem)


def _sort_kernel(pos_ref, src_hbm, dst_hbm, buf, lsem, rsem, *, s):
    i = pl.program_id(0)
    n = pl.num_programs(0)
    tp = pos_ref.shape[2]
    rows = tp * s
    slot = i % PERM_SLOTS

    def tile_load(step, sl):
        return pltpu.make_async_copy(src_hbm.at[pl.ds(pl.multiple_of(step * rows, rows), rows), :], buf.at[sl],
                                     lsem.at[sl])

    def wait_rows(sl):
        for _ in range(TOP_K):
            pltpu.make_async_copy(buf.at[sl], dst_hbm.at[pl.ds(0, rows), :], rsem.at[sl]).wait()

    @pl.when(i == 0)
    def _():
        tile_load(0, 0).start()

    @pl.when(i + 1 < n)
    def _():
        tile_load(i + 1, (i + 1) % PERM_SLOTS).start()

    tile_load(i, slot).wait()

    def body(r, c):
        for k in range(TOP_K):
            _token_copy(buf.at[slot], r, dst_hbm, pos_ref[0, k, r], s, rsem.at[slot]).start()
        return c
    lax.fori_loop(0, tp, body, 0, unroll=8)

    @pl.when(i >= 1)
    def _():
        wait_rows((i - 1) % PERM_SLOTS)

    @pl.when(i == n - 1)
    def _():
        wait_rows(slot)


def _unsort_kernel(pos_ref, src_hbm, dst_hbm, buf, rsem, wsem, *, s, n_blocks):
    i = pl.program_id(0)
    tp = pos_ref.shape[2]
    rows = tp * s
    slot = i % PERM_SLOTS

    def write_back(step, sl, k):
        return pltpu.make_async_copy(buf.at[sl, k], dst_hbm.at[k, pl.ds(pl.multiple_of(step * rows, rows), rows), :],
                                     wsem.at[sl])

    def wait_rows(sl):
        for k in range(TOP_K):
            pltpu.make_async_copy(src_hbm.at[pl.ds(0, rows), :], buf.at[sl, k], rsem.at[sl]).wait()

    def finish(step, sl):
        wait_rows(sl)
        for k in range(TOP_K):
            write_back(step, sl, k).start()

    def wait_write_back(step, sl):
        for k in range(TOP_K):
            write_back(step, sl, k).wait()

    @pl.when(i >= PERM_SLOTS)
    def _():
        wait_write_back(i - PERM_SLOTS, slot)

    def body(r, c):
        for k in range(TOP_K):
            _token_copy(src_hbm, pos_ref[0, k, r], buf.at[slot, k], r, s, rsem.at[slot]).start()
        return c
    lax.fori_loop(0, tp, body, 0, unroll=8)

    @pl.when(i >= 1)
    def _():
        finish(i - 1, (i - 1) % PERM_SLOTS)

    @pl.when(i == n_blocks - 1)
    def _():
        finish(i, slot)
        for back in range(min(PERM_SLOTS, n_blocks)):
            wait_write_back(i - back, (i - back) % PERM_SLOTS)


def _sort_rows(pos_blocks, src, *, n_tok, s):
    n_blocks, _, tp = pos_blocks.shape
    return pl.pallas_call(
        functools.partial(_sort_kernel, s=s),
        grid=(n_blocks,),
        in_specs=[pl.BlockSpec((1, TOP_K, tp), lambda i: (i, 0, 0), memory_space=pltpu.SMEM),
                  pl.BlockSpec(memory_space=pl.ANY)],
        out_specs=pl.BlockSpec(memory_space=pl.ANY),
        out_shape=jax.ShapeDtypeStruct((TOP_K * n_tok * s, LANES), src.dtype),
        scratch_shapes=[pltpu.VMEM((PERM_SLOTS, tp * s, LANES), src.dtype),
                        pltpu.SemaphoreType.DMA((PERM_SLOTS,)), pltpu.SemaphoreType.DMA((PERM_SLOTS,))],
        compiler_params=_params("arbitrary"),
        name="moe_sort_rows",
    )(pos_blocks, src)


def _unsort_rows(pos_blocks, src, *, n_tok, s):
    n_blocks, _, tp = pos_blocks.shape
    return pl.pallas_call(
        functools.partial(_unsort_kernel, s=s, n_blocks=n_blocks),
        grid=(n_blocks,),
        in_specs=[pl.BlockSpec((1, TOP_K, tp), lambda i: (i, 0, 0), memory_space=pltpu.SMEM),
                  pl.BlockSpec(memory_space=pl.ANY)],
        out_specs=pl.BlockSpec(memory_space=pl.ANY),
        out_shape=jax.ShapeDtypeStruct((TOP_K, n_tok * s, LANES), src.dtype),
        scratch_shapes=[pltpu.VMEM((PERM_SLOTS, TOP_K, tp * s, LANES), src.dtype),
                        pltpu.SemaphoreType.DMA((PERM_SLOTS,)), pltpu.SemaphoreType.DMA((PERM_SLOTS,))],
        compiler_params=_params("arbitrary"),
        name="moe_unsort_rows",
    )(pos_blocks, src)


def _gmm_kernel(ie_ref, it_ref, lo_ref, hi_ref, first_ref, n_ref, xs_ref, wg_ref, wu_ref, wd_ref, ys_ref,
                wg_b, wu_b, wd_b):
    w = pl.program_id(0)

    @pl.when(w < n_ref[0])
    def _():
        @pl.when((w == 0) | (ie_ref[w] != ie_ref[jnp.maximum(w - 1, 0)]))
        def _():
            wg_b[...] = wg_ref[...].astype(BF16)
            wu_b[...] = wu_ref[...].astype(BF16)
            wd_b[...] = wd_ref[...].astype(BF16)

        half = wg_b.shape[0] // 2
        s = half // LANES
        x_lo, x_hi = _unpack_pair(_load_token_tiles(xs_ref, s))
        x_lo, x_hi = x_lo.astype(BF16), x_hi.astype(BF16)
        hg = _dot(x_lo, wg_b[0:half, :]) + _dot(x_hi, wg_b[half:, :])
        hu = _dot(x_lo, wu_b[0:half, :]) + _dot(x_hi, wu_b[half:, :])
        yp = _pack_row(_dot((_silu(hg) * hu).astype(BF16), wd_b[...]))
        rows = lax.broadcasted_iota(I32, yp.shape, 0)
        mine = (rows >= lo_ref[w]) & (rows < hi_ref[w])

        @pl.when(first_ref[w] == 1)
        def _():
            _store_token_tiles(ys_ref, jnp.where(mine, yp, jnp.uint32(0)))

        @pl.when(first_ref[w] == 0)
        def _():
            _store_token_tiles(ys_ref, jnp.where(mine, yp, _load_token_tiles(ys_ref, s)))


def _moe_experts(xs, plan, layer, wg, wu, wd):
    _, _, d, de = wg.shape
    s = d // 2 // LANES
    tg = GROUP_TILE
    n_items_max = plan[0].shape[0]
    grid_spec = pltpu.PrefetchScalarGridSpec(
        num_scalar_prefetch=6,
        grid=(n_items_max,),
        in_specs=[
            pl.BlockSpec((tg * s, LANES), lambda w, ie, it, lo, hi, fi, n: (it[w], 0)),
            pl.BlockSpec((None, None, d, de), lambda w, ie, it, lo, hi, fi, n: (layer, ie[w], 0, 0)),
            pl.BlockSpec((None, None, d, de), lambda w, ie, it, lo, hi, fi, n: (layer, ie[w], 0, 0)),
            pl.BlockSpec((None, None, de, d), lambda w, ie, it, lo, hi, fi, n: (layer, ie[w], 0, 0)),
        ],
        out_specs=pl.BlockSpec((tg * s, LANES), lambda w, ie, it, lo, hi, fi, n: (it[w], 0)),
        scratch_shapes=[pltpu.VMEM((d, de), BF16), pltpu.VMEM((d, de), BF16), pltpu.VMEM((de, d), BF16)],
    )
    return pl.pallas_call(
        _gmm_kernel,
        grid_spec=grid_spec,
        out_shape=jax.ShapeDtypeStruct(xs.shape, U32),
        compiler_params=_params("arbitrary"),
        name="moe_experts",
    )(*plan, xs, wg, wu, wd)


def _moe_combine(y0_ref, y1_ref, wts_ref, s):
    wts = wts_ref[...]
    return (wts[:, 0:1] * _unpack_row(_load_token_tiles(y0_ref, s))
            + wts[:, 1:2] * _unpack_row(_load_token_tiles(y1_ref, s)))


def _inproj_kernel(x_ref, y0_ref, y1_ref, wts_ref, mp_ref, m_ref, nw_ref, w_ref, wdt_ref, cw_ref, cb_ref, dtb_ref,
                   lat_ref, z_ref, xbc_ref, dt_ref, h_ref, mask_ref, *, nz, n_lat_tiles, ctx_row):
    i = pl.program_id(0)
    j = pl.program_id(1)
    n_sub = w_ref.shape[1] // COL_TILE

    @pl.when(j == 0)
    def _():
        s = x_ref.shape[1] // 2 // LANES
        lat = x_ref[...] + mp_ref[5:6, :] * _moe_combine(y0_ref, y1_ref, wts_ref, s)
        lat_ref[...] = lat
        h_ref[...] = _mod_norm(lat, nw_ref[...], m_ref[0:1, :], m_ref[1:2, :]).astype(BF16)
        _conv_masks(mask_ref, jnp.where(i < n_lat_tiles, GRID_W, ctx_row))

    @pl.when(j < nz)
    def _():
        for s in range(n_sub):
            sl = slice(s * COL_TILE, (s + 1) * COL_TILE)
            z_ref[:, sl] = _dot(h_ref[...], w_ref[:, sl]).astype(BF16)

    @pl.when(j >= nz)
    def _():
        for s in range(n_sub):
            sl = slice(s * COL_TILE, (s + 1) * COL_TILE)
            r = _dot(h_ref[...], w_ref[:, sl])
            xbc_ref[:, sl] = _silu(_dwconv_rows(r, cw_ref, mask_ref, s * COL_TILE) + cb_ref[:, sl]).astype(BF16)

    @pl.when(j == pl.num_programs(1) - 1)
    def _():
        v = _dot(h_ref[...], wdt_ref[...]) + dtb_ref[...]
        dt_ref[...] = jnp.maximum(v, 0.0) + jnp.log1p(jnp.exp(-jnp.abs(v)))


def _ssd_in_proj(tokens, y2, wts, mods, layer, nw, w_zx, w_dt, conv_w, conv_b, dt_bias,
                 *, seq, n_lat_tiles, ctx_row, d_inner, conv_dim):
    t, d = tokens.shape
    tm = TOKEN_TILE
    tc = next(c for c in (4 * COL_TILE, 2 * COL_TILE, COL_TILE) if d_inner % c == 0 and conv_dim % c == 0)
    nz, nx = d_inner // tc, conv_dim // tc
    mod_row = _mod_row_fn(mods, seq, tm)

    def xcol(j):
        return jnp.maximum(j - nz, 0)

    kern = functools.partial(_inproj_kernel, nz=nz, n_lat_tiles=n_lat_tiles, ctx_row=ctx_row)
    return pl.pallas_call(
        kern,
        grid=(t // tm, nz + nx),
        in_specs=[
            pl.BlockSpec((tm, d), lambda i, j: (i, 0)),
            pl.BlockSpec((None, tm * (d // 2 // LANES), LANES), lambda i, j: (0, i, 0)),
            pl.BlockSpec((None, tm * (d // 2 // LANES), LANES), lambda i, j: (1, i, 0)),
            pl.BlockSpec((tm, LANES), lambda i, j: (i, 0)),
            pl.BlockSpec((None, None, 6, d), lambda i, j: (layer - 1, mod_row(i), 0, 0)),
            pl.BlockSpec((None, None, 6, d), lambda i, j: (layer, mod_row(i), 0, 0)),
            pl.BlockSpec((1, d), lambda i, j: (0, 0)),
            pl.BlockSpec((d, tc), lambda i, j: (0, j)),
            pl.BlockSpec((d, LANES), lambda i, j: (0, 0)),
            pl.BlockSpec((conv_w.shape[0], tc), lambda i, j: (0, xcol(j))),
            pl.BlockSpec((1, tc), lambda i, j: (0, xcol(j))),
            pl.BlockSpec((1, LANES), lambda i, j: (0, 0)),
        ],
        out_specs=[
            pl.BlockSpec((tm, d), lambda i, j: (i, 0)),
            pl.BlockSpec((tm, tc), lambda i, j: (i, jnp.minimum(j, nz - 1))),
            pl.BlockSpec((tm, tc), lambda i, j: (i, xcol(j))),
            pl.BlockSpec((tm, LANES), lambda i, j: (i, 0)),
        ],
        out_shape=[
            jax.ShapeDtypeStruct((t, d), F32),
            jax.ShapeDtypeStruct((t, d_inner), BF16),
            jax.ShapeDtypeStruct((t, conv_dim), BF16),
            jax.ShapeDtypeStruct((t, LANES), F32),
        ],
        scratch_shapes=[pltpu.VMEM((tm, d), BF16), pltpu.VMEM((conv_w.shape[0] - 1, tm, LANES), F32)],
        compiler_params=_params("arbitrary", "arbitrary"),
        name="ssd_in_proj",
    )(tokens, y2, y2, wts, mods, mods, nw, w_zx, w_dt, conv_w, conv_b, dt_bias)


def _ssd_kernel(x_ref, b_ref, c_ref, dt_ref, a_ref, y_ref, st_ref, *, n_heads):
    d = pl.program_id(1)
    k = pl.program_id(2)
    q = x_ref.shape[0]
    n_groups = st_ref.shape[0]
    gw = st_ref.shape[2]
    hpg = gw // HEAD_DIM
    pw = 2 * HEAD_DIM
    fwd = d == 0

    @pl.when(k == 0)
    def _():
        st_ref[...] = jnp.zeros_like(st_ref)

    t_i = lax.broadcasted_iota(I32, (q, q), 0)
    s_i = lax.broadcasted_iota(I32, (q, q), 1)
    incl = jnp.where(fwd, t_i - s_i, s_i - t_i) >= 0

    dt_all = dt_ref[...]
    dta_all = dt_all * (a_ref[...] * LOG2_E)
    cum_all = jnp.dot(jnp.where(incl, 1.0, 0.0), dta_all, preferred_element_type=F32,
                      precision=lax.Precision.HIGHEST)
    cum_t_all = cum_all.T
    dt_t_all = dt_all.T
    tot_t_all = jnp.sum(dta_all.T, axis=1, keepdims=True)

    def pick_cols(a):
        return jnp.where(fwd, a[:, 0:n_heads], a[:, n_heads:2 * n_heads])

    def pick_rows(a):
        return jnp.where(fwd, a[0:n_heads, :], a[n_heads:2 * n_heads, :])

    cum = pick_cols(cum_all)
    tot = pick_cols(jnp.sum(dta_all, axis=0, keepdims=True))
    cum_t = pick_rows(cum_t_all)
    dt_t = pick_rows(dt_t_all)
    to_end_t = (dt_t * jnp.exp2(pick_rows(tot_t_all) - cum_t)).astype(BF16)
    dt_tb = dt_t.astype(BF16)
    dec_all = jnp.exp2(tot)

    low_half = lax.broadcasted_iota(I32, (q, pw), 1) < HEAD_DIM
    low_row = lax.broadcasted_iota(I32, (1, pw), 1) < HEAD_DIM

    for g in range(n_groups):
        bg = b_ref[:, g * D_STATE:(g + 1) * D_STATE]
        cg = c_ref[:, g * D_STATE:(g + 1) * D_STATE]
        cb = lax.dot_general(cg, bg, (((1,), (1,)), ((), ())), preferred_element_type=F32)
        cb = jnp.where(incl, cb, 0.0).astype(BF16)
        bg_t = bg.astype(F32).T.astype(BF16)
        st = st_ref[g]
        y_off = _dot(cg, st.astype(BF16))

        for pair in range(hpg // 2):
            psl = slice(pair * pw, (pair + 1) * pw)
            xp = x_ref[:, g * gw + pair * pw:g * gw + (pair + 1) * pw]
            zero = jnp.zeros_like(xp)
            rhs = jnp.concatenate([jnp.where(low_half, xp, zero), jnp.where(low_half, zero, xp)], axis=0)
            w_parts, u_parts, cols = [], [], []
            for hh in range(2):
                h = g * hpg + pair * 2 + hh
                col = jnp.broadcast_to(cum[:, h:h + 1], (q, q))
                seg = jnp.exp2(jnp.minimum(col - cum_t[h:h + 1, :], 0.0))
                w_parts.append(cb * seg.astype(BF16) * dt_tb[h:h + 1, :])
                u_parts.append(bg_t * to_end_t[h:h + 1, :])
                cols.append(col)
            y_diag = _dot(jnp.concatenate(w_parts, axis=1), rhs)
            upd = _dot(jnp.concatenate(u_parts, axis=1), rhs)
            dec_in = jnp.exp2(jnp.where(low_half, cols[0], cols[1]))
            y_ref[:, g * gw + pair * pw:g * gw + (pair + 1) * pw] = (y_diag + y_off[:, psl] * dec_in).astype(BF16)
            h0 = g * hpg + pair * 2
            dec_pair = jnp.where(low_row, jnp.broadcast_to(dec_all[:, h0:h0 + 1], (1, pw)),
                                 jnp.broadcast_to(dec_all[:, h0 + 1:h0 + 2], (1, pw)))
            st_ref[g, :, psl] = st[:, psl] * dec_pair + upd


def _ssd_scan(xbc, dt, a_neg, *, n_batch, lat_chunks, ctx_chunks, d_inner, n_heads):
    t = xbc.shape[0]
    q = CHUNK
    gn = SSD_GROUPS * D_STATE
    gw = d_inner // SSD_GROUPS
    assert gw % (2 * HEAD_DIM) == 0 and d_inner % gn == 0 and q == 2 * HEAD_DIM and q == LANES
    n_steps = ctx_chunks + lat_chunks
    ctx_base = n_batch * lat_chunks

    def chunk(b, d, k):
        kc = jnp.where(d == 0, k, ctx_chunks - 1 - k)
        kl = jnp.where(d == 0, k - ctx_chunks, n_steps - 1 - k)
        return jnp.where(k < ctx_chunks, ctx_base + b * ctx_chunks + kc, b * lat_chunks + kl)

    kern = functools.partial(_ssd_kernel, n_heads=n_heads)
    return pl.pallas_call(
        kern,
        grid=(n_batch, 2, n_steps),
        in_specs=[
            pl.BlockSpec((q, d_inner), lambda b, d, k: (chunk(b, d, k), 0)),
            pl.BlockSpec((q, gn), lambda b, d, k: (chunk(b, d, k), d_inner // gn)),
            pl.BlockSpec((q, gn), lambda b, d, k: (chunk(b, d, k), d_inner // gn + 1)),
            pl.BlockSpec((q, LANES), lambda b, d, k: (chunk(b, d, k), 0)),
            pl.BlockSpec((1, LANES), lambda b, d, k: (0, 0)),
        ],
        out_specs=pl.BlockSpec((None, q, d_inner), lambda b, d, k: (d, chunk(b, d, k), 0)),
        out_shape=jax.ShapeDtypeStruct((2, t, d_inner), BF16),
        scratch_shapes=[pltpu.VMEM((SSD_GROUPS, D_STATE, gw), F32)],
        compiler_params=_params("arbitrary", "arbitrary", "arbitrary"),
        name="ssd_scan",
    )(xbc, xbc, xbc, dt, a_neg)


def _outproj_kernel(yf_ref, yb_ref, xs_ref, z_ref, dsk_ref, gnw_ref, lat_ref, m_ref, w_ref, o_ref):
    gw = w_ref.shape[0] // SSD_GROUPS
    acc = None
    for g in range(SSD_GROUPS):
        sl = slice(g * gw, (g + 1) * gw)
        y = (yf_ref[:, sl].astype(F32) + yb_ref[:, sl].astype(F32)
             + dsk_ref[:, sl] * xs_ref[:, sl].astype(F32))
        gated = y * _silu(z_ref[:, sl].astype(F32))
        ms = jnp.mean(gated * gated, axis=-1, keepdims=True)
        gn = (gated * lax.rsqrt(ms + EPS) * gnw_ref[:, sl]).astype(BF16)
        part = _dot(gn, w_ref[sl, :])
        acc = part if acc is None else acc + part
    o_ref[...] = lat_ref[...] + m_ref[2:3, :] * acc


def _ssd_out_proj(y, xbc, z, d_skip, gnw, tokens, mods, layer, w_out, *, n_tok, seq):
    d_inner, d = w_out.shape
    tm = OUT_TOKEN_TILE
    mod_row = _mod_row_fn(mods, seq, tm)
    return pl.pallas_call(
        _outproj_kernel,
        grid=(n_tok // tm,),
        in_specs=[
            pl.BlockSpec((None, tm, d_inner), lambda i: (0, i, 0)),
            pl.BlockSpec((None, tm, d_inner), lambda i: (1, i, 0)),
            pl.BlockSpec((tm, d_inner), lambda i: (i, 0)),
            pl.BlockSpec((tm, d_inner), lambda i: (i, 0)),
            pl.BlockSpec((1, d_inner), lambda i: (0, 0)),
            pl.BlockSpec((1, d_inner), lambda i: (0, 0)),
            pl.BlockSpec((tm, d), lambda i: (i, 0)),
            pl.BlockSpec((None, None, 6, d), lambda i: (layer, mod_row(i), 0, 0)),
            pl.BlockSpec((d_inner, d), lambda i: (0, 0), pipeline_mode=pl.Buffered(1)),
        ],
        out_specs=pl.BlockSpec((tm, d), lambda i: (i, 0)),
        out_shape=jax.ShapeDtypeStruct((n_tok, d), F32),
        compiler_params=_params("arbitrary"),
        name="ssd_out_proj",
    )(y, y, xbc, z, d_skip, gnw, tokens, mods, w_out)


def _final_kernel(x_ref, y0_ref, y1_ref, wts_ref, m_ref, nw_ref, o_ref):
    s = x_ref.shape[1] // 2 // LANES
    lat = x_ref[...] + m_ref[5:6, :] * _moe_combine(y0_ref, y1_ref, wts_ref, s)
    ms = jnp.mean(lat * lat, axis=-1, keepdims=True)
    o_ref[...] = lat * lax.rsqrt(ms + EPS) * nw_ref[...]


def _final_norm(tokens, y2, wts, mods, layer, nw, *, seq):
    t, d = tokens.shape
    tm = TOKEN_TILE
    mod_row = _mod_row_fn(mods, seq, tm)
    return pl.pallas_call(
        _final_kernel,
        grid=(t // tm,),
        in_specs=[
            pl.BlockSpec((tm, d), lambda i: (i, 0)),
            pl.BlockSpec((None, tm * (d // 2 // LANES), LANES), lambda i: (0, i, 0)),
            pl.BlockSpec((None, tm * (d // 2 // LANES), LANES), lambda i: (1, i, 0)),
            pl.BlockSpec((tm, LANES), lambda i: (i, 0)),
            pl.BlockSpec((None, None, 6, d), lambda i: (layer, mod_row(i), 0, 0)),
            pl.BlockSpec((1, d), lambda i: (0, 0)),
        ],
        out_specs=pl.BlockSpec((tm, d), lambda i: (i, 0)),
        out_shape=jax.ShapeDtypeStruct((t, d), F32),
        compiler_params=_params("arbitrary"),
        name="final_norm",
    )(tokens, y2, y2, wts, mods, nw)


def _router_weights(rg_w, rg_b, re_w, re_b):
    d, n_groups = rg_w.shape
    n_exp = re_w.shape[1]
    pad = LANES - n_groups - n_exp
    wr = jnp.concatenate([rg_w, re_w, jnp.zeros((d, pad), F32)], axis=1).astype(BF16)
    br = jnp.concatenate([rg_b, re_b, jnp.zeros((pad,), F32)]).reshape(1, LANES)
    return wr, br


def _moe(tokens, n_tok, mods, layer, nw, rg_w, rg_b, re_w, re_b, w_gate, w_up, w_down, *, seq):
    n_groups = rg_w.shape[1]
    n_exp = re_w.shape[1]
    s = tokens.shape[1] // 2 // LANES
    assert n_tok % PERM_TILE == 0 and (TOP_K * n_tok) % GROUP_TILE == 0 and tokens.shape[1] % (2 * LANES) == 0
    wr, br = _router_weights(rg_w, rg_b, re_w, re_b)
    fp, meta, wts, counts = _moe_route(tokens, n_tok, mods, layer, nw, wr, br, seq=seq, n_groups=n_groups,
                                       per_group=n_exp // n_groups)
    pos_blocks, plan = _route_plan(meta, counts, n_tok=n_tok, n_exp=n_exp, n_groups=n_groups,
                                   tg=GROUP_TILE, tp=PERM_TILE)
    xs = _sort_rows(pos_blocks, fp, n_tok=n_tok, s=s)
    ys = _moe_experts(xs, plan, layer, w_gate, w_up, w_down)
    return _unsort_rows(pos_blocks, ys, n_tok=n_tok, s=s), wts


def kernel(x, c, ctx, c_ctx, ada_w, ada_b, norm_mix_w, norm_ffn_w, sc_w_in, sc_conv_w, sc_w_out, ssd_w_in,
           ssd_conv_w, ssd_conv_b, ssd_dt_bias, ssd_a_log, ssd_d, ssd_norm_w, ssd_w_out, rg_w, rg_b, re_w, re_b,
           moe_w_gate, moe_w_up, moe_w_down, final_norm_w):
    n_batch, seq, d = x.shape
    ctx_len = ctx.shape[1]
    depth = ada_w.shape[0]
    d_inner = ssd_w_out.shape[1]
    conv_dim = ssd_conv_w.shape[2]
    n_heads = ssd_d.shape[1]
    tm = TOKEN_TILE
    t_lat, t_ctx = n_batch * seq, n_batch * ctx_len
    assert depth == 2 and seq % tm == 0 and t_ctx % tm == 0 and tm % ctx_len == 0 and tm % GRID_W == 0
    assert ctx_len & (ctx_len - 1) == 0 and GRID_W & (GRID_W - 1) == 0
    assert seq % CHUNK == 0 and ctx_len % CHUNK == 0 and d_inner == n_heads * HEAD_DIM
    assert n_batch + 1 <= SUBLANES and 2 * n_heads <= LANES and d_inner % COL_TILE == 0 and conv_dim % COL_TILE == 0
    n_lat_tiles = t_lat // tm

    cvec = jnp.zeros((SUBLANES, d), F32).at[0:n_batch].set(c).at[n_batch].set(c_ctx)
    mods = _ada_mods(cvec, ada_w, ada_b)

    x_lat = x.reshape(t_lat, d)
    x_ctx = ctx.reshape(t_ctx, d)

    tokens = _short_conv_layer(x_lat, x_ctx, mods, 0, norm_mix_w[0:1], sc_w_in[0].astype(BF16), sc_conv_w[0],
                               sc_w_out[0].astype(BF16), seq=seq, ctx_row=ctx_len)
    t_all = t_lat + t_ctx
    y2, wts = _moe(tokens, t_all, mods, 0, norm_ffn_w[0:1], rg_w[0], rg_b[0], re_w[0], re_b[0],
                   moe_w_gate, moe_w_up, moe_w_down, seq=seq)

    w_in = ssd_w_in[0]
    dt_cols = 2 * n_heads
    w_zx = w_in[:, :d_inner + conv_dim].astype(BF16)
    w_dt = jnp.concatenate([w_in[:, d_inner + conv_dim:], jnp.zeros((d, LANES - dt_cols), F32)],
                           axis=1).astype(BF16)
    dt_bias = jnp.zeros((1, LANES), F32).at[0, :dt_cols].set(ssd_dt_bias[0].reshape(-1))
    a_neg = jnp.zeros((1, LANES), F32).at[0, :dt_cols].set(-jnp.exp(ssd_a_log[0].reshape(-1)))
    tokens, z, xbc, dt = _ssd_in_proj(tokens, y2, wts, mods, 1, norm_mix_w[1:2], w_zx, w_dt, ssd_conv_w[0],
                                      ssd_conv_b[0].reshape(1, conv_dim), dt_bias, seq=seq,
                                      n_lat_tiles=n_lat_tiles, ctx_row=ctx_len, d_inner=d_inner, conv_dim=conv_dim)
    y = _ssd_scan(xbc, dt, a_neg, n_batch=n_batch, lat_chunks=seq // CHUNK, ctx_chunks=ctx_len // CHUNK,
                  d_inner=d_inner, n_heads=n_heads)
    d_skip = jnp.repeat(ssd_d[0], HEAD_DIM).reshape(1, d_inner)
    lat = _ssd_out_proj(y, xbc, z, d_skip, ssd_norm_w[0].reshape(1, d_inner), tokens, mods, 1,
                        ssd_w_out[0].astype(BF16), n_tok=t_lat, seq=seq)

    y2, wts = _moe(lat, t_lat, mods, 1, norm_ffn_w[1:2], rg_w[1], rg_b[1], re_w[1], re_b[1],
                   moe_w_gate, moe_w_up, moe_w_down, seq=seq)
    out = _final_norm(lat, y2, wts, mods, 1, final_norm_w.reshape(1, d), seq=seq)
    return out.reshape(n_batch, seq, d)
```

```python
import functools

import jax
import jax.numpy as jnp
from jax import lax
from jax.experimental import pallas as pl
from jax.experimental.pallas import tpu as pltpu

F32 = jnp.float32
BF16 = jnp.bfloat16
I32 = jnp.int32
U32 = jnp.uint32

EPS = 1e-6
GRID_W = 64
HEAD_DIM = 64
D_STATE = 128
SSD_GROUPS = 8
CHUNK = 128
TOP_K = 2

LANES = 128
SUBLANES = 8
VMEM_LIMIT = 58 * 1024 * 1024
TOKEN_TILE = 512
COL_TILE = 512
OUT_TOKEN_TILE = 256
GROUP_TILE = 256
PERM_TILE = 512
ADA_COL_TILE = 1024
HI_MASK = 0xFFFF0000
LOG2_E = 1.4426950408889634


def _params(*sem):
    return pltpu.CompilerParams(dimension_semantics=sem, vmem_limit_bytes=VMEM_LIMIT)


def _dot(a, b):
    return jnp.dot(a, b, preferred_element_type=F32)


def _silu(x):
    return x * jax.nn.sigmoid(x)


def _mod_norm(x, nw, shift, scale):
    ms = jnp.mean(x * x, axis=-1, keepdims=True)
    y = x * lax.rsqrt(ms + EPS)
    return (y * nw) * (1.0 + scale) + shift


def _pack_pair(lo, hi):
    lo_bits = lax.bitcast_convert_type(lo.astype(BF16).astype(F32), U32)
    hi_bits = lax.bitcast_convert_type(hi.astype(BF16).astype(F32), U32)
    return (lo_bits >> 16) | (hi_bits & jnp.uint32(HI_MASK))


def _unpack_pair(u):
    lo = lax.bitcast_convert_type(u << 16, F32)
    hi = lax.bitcast_convert_type(u & jnp.uint32(HI_MASK), F32)
    return lo, hi


def _pack_row(x):
    half = x.shape[1] // 2
    return _pack_pair(x[:, :half], x[:, half:])


def _unpack_row(u):
    lo, hi = _unpack_pair(u)
    return jnp.concatenate([lo, hi], axis=1)


def _store_token_tiles(ref, packed):
    rows, width = packed.shape
    s = width // LANES
    for c in range(s):
        ref[pl.ds(c, rows, stride=s), :] = packed[:, c * LANES:(c + 1) * LANES]


def _load_token_tiles(ref, s):
    rows = ref.shape[0] // s
    return jnp.concatenate([ref[pl.ds(c, rows, stride=s), :] for c in range(s)], axis=1)


def _conv_masks(mask_ref, row_len):
    n_side, tm, _ = mask_ref.shape
    half = n_side // 2
    pos = lax.broadcasted_iota(I32, (tm, LANES), 0) & (row_len - 1)
    for m in range(n_side):
        off = m - half if m < half else m - half + 1
        valid = (pos + off >= 0) & (pos + off < row_len)
        mask_ref[m] = jnp.where(valid, 1.0, 0.0)


def _dwconv_rows(u, cw_ref, mask_ref, col0):
    tm, width_c = u.shape
    width = cw_ref.shape[0]
    half = width // 2
    outs = []
    for c in range(width_c // LANES):
        sl = slice(c * LANES, (c + 1) * LANES)
        wsl = slice(col0 + c * LANES, col0 + (c + 1) * LANES)
        uc = u[:, sl]
        out = cw_ref[half:half + 1, wsl] * uc
        for k in range(width):
            off = k - half
            if off == 0:
                continue
            m = k if k < half else k - 1
            shifted = pltpu.roll(uc, (-off) % tm, axis=0)
            out = out + cw_ref[k:k + 1, wsl] * (shifted * mask_ref[m])
        outs.append(out)
    return outs[0] if len(outs) == 1 else jnp.concatenate(outs, axis=1)


def _ada_kernel(c_ref, w_ref, b_ref, o_ref):
    s = _silu(c_ref[...])
    o_ref[...] = jnp.dot(s, w_ref[...], preferred_element_type=F32,
                         precision=lax.Precision.HIGHEST) + b_ref[...]


def _ada_mods(cvec, ada_w, ada_b):
    depth, d, d6 = ada_w.shape
    tn = min(ADA_COL_TILE, d6)
    out = pl.pallas_call(
        _ada_kernel,
        grid=(depth, d6 // tn),
        in_specs=[pl.BlockSpec((SUBLANES, d), lambda l, j: (0, 0)),
                  pl.BlockSpec((None, d, tn), lambda l, j: (l, 0, j)),
                  pl.BlockSpec((None, 1, tn), lambda l, j: (l, 0, j))],
        out_specs=pl.BlockSpec((None, SUBLANES, tn), lambda l, j: (l, 0, j)),
        out_shape=jax.ShapeDtypeStruct((depth, SUBLANES, d6), F32),
        compiler_params=_params("arbitrary", "arbitrary"),
        name="ada_mods",
    )(cvec, ada_w, ada_b.reshape(depth, 1, d6))
    return out.reshape(depth, SUBLANES, 6, d)


def _mod_row_fn(mods, tokens_per_batch, tile):
    n_rows = mods.shape[1]
    per_batch = tokens_per_batch // tile

    def mod_row(i):
        return jnp.minimum(i // per_batch, n_rows - 1)
    return mod_row


def _sc_kernel(xl_ref, xc_ref, m_ref, nw_ref, wb_ref, wc_ref, wv_ref, cw_ref, wo_ref,
               o_ref, h_ref, acc_ref, mask_ref, *, n_lat_tiles, ctx_row):
    i = pl.program_id(0)
    j = pl.program_id(1)
    is_lat = i < n_lat_tiles

    def for_source(fn):
        pl.when(is_lat)(lambda: fn(xl_ref))
        pl.when(jnp.logical_not(is_lat))(lambda: fn(xc_ref))

    @pl.when(j == 0)
    def _():
        def prologue(x_ref):
            h_ref[...] = _mod_norm(x_ref[...], nw_ref[...], m_ref[0:1, :], m_ref[1:2, :]).astype(BF16)
        for_source(prologue)
        acc_ref[...] = jnp.zeros_like(acc_ref)
        _conv_masks(mask_ref, jnp.where(is_lat, GRID_W, ctx_row))

    h = h_ref[...]
    gate_b = _dot(h, wb_ref[...])
    u = _dot(h, wc_ref[...]) * _dot(h, wv_ref[...])
    g = (gate_b * _dwconv_rows(u, cw_ref, mask_ref, 0)).astype(BF16)
    acc_ref[...] += _dot(g, wo_ref[...])

    @pl.when(j == pl.num_programs(1) - 1)
    def _():
        def epilogue(x_ref):
            o_ref[...] = x_ref[...] + m_ref[2:3, :] * acc_ref[...]
        for_source(epilogue)


def _short_conv_layer(x_lat, x_ctx, mods, layer, nw, w_in, conv_w, w_out, *, seq, ctx_row):
    t_lat, d = x_lat.shape
    t_ctx = x_ctx.shape[0]
    tm, tc = TOKEN_TILE, min(COL_TILE, d)
    n_lat, n_ctx, nj = t_lat // tm, t_ctx // tm, d // tc
    mod_row = _mod_row_fn(mods, seq, tm)
    kern = functools.partial(_sc_kernel, n_lat_tiles=n_lat, ctx_row=ctx_row)
    return pl.pallas_call(
        kern,
        grid=(n_lat + n_ctx, nj),
        in_specs=[
            pl.BlockSpec((tm, d), lambda i, j: (jnp.minimum(i, n_lat - 1), 0)),
            pl.BlockSpec((tm, d), lambda i, j: (jnp.maximum(i - n_lat, 0), 0)),
            pl.BlockSpec((None, None, 6, d), lambda i, j: (layer, mod_row(i), 0, 0)),
            pl.BlockSpec((1, d), lambda i, j: (0, 0)),
            pl.BlockSpec((d, tc), lambda i, j: (0, j)),
            pl.BlockSpec((d, tc), lambda i, j: (0, nj + j)),
            pl.BlockSpec((d, tc), lambda i, j: (0, 2 * nj + j)),
            pl.BlockSpec((conv_w.shape[0], tc), lambda i, j: (0, j)),
            pl.BlockSpec((tc, d), lambda i, j: (j, 0)),
        ],
        out_specs=pl.BlockSpec((tm, d), lambda i, j: (i, 0)),
        out_shape=jax.ShapeDtypeStruct((t_lat + t_ctx, d), F32),
        scratch_shapes=[pltpu.VMEM((tm, d), BF16), pltpu.VMEM((tm, d), F32),
                        pltpu.VMEM((conv_w.shape[0] - 1, tm, LANES), F32)],
        compiler_params=_params("arbitrary", "arbitrary"),
        name="short_conv_mixer",
    )(x_lat, x_ctx, mods, nw, w_in, w_in, w_in, conv_w, w_out)


def _route_kernel(x_ref, m_ref, nw_ref, wr_ref, br_ref, fp_ref, meta_ref, wts_ref, cnt_ref, run_ref,
                  *, n_groups, per_group):
    i = pl.program_id(0)

    @pl.when(i == 0)
    def _():
        run_ref[...] = jnp.zeros_like(run_ref)

    f = _mod_norm(x_ref[...], nw_ref[...], m_ref[3:4, :], m_ref[4:5, :])
    _store_token_tiles(fp_ref, _pack_row(f))
    logits = _dot(f.astype(BF16), wr_ref[...]) + br_ref[...]
    tm = logits.shape[0]
    lane = lax.broadcasted_iota(I32, logits.shape, 1).astype(F32)
    neg = -jnp.inf

    g_mask = lane < n_groups
    gl = jnp.where(g_mask, logits, neg)
    g_max = jnp.max(gl, axis=1, keepdims=True)
    g_sel = jnp.min(jnp.where(gl == g_max, lane, LANES), axis=1, keepdims=True)
    p_group = 1.0 / jnp.sum(jnp.where(g_mask, jnp.exp(gl - g_max), 0.0), axis=1, keepdims=True)

    e_lo = n_groups + g_sel * per_group
    in_group = (lane >= e_lo) & (lane < e_lo + per_group)
    el = jnp.where(in_group, logits, neg)
    e_max = jnp.max(el, axis=1, keepdims=True)
    ex = jnp.where(in_group, jnp.exp(el - e_max), 0.0)
    p = jnp.where(in_group, ex / jnp.sum(ex, axis=1, keepdims=True), -1.0)
    p1 = jnp.max(p, axis=1, keepdims=True)
    l1 = jnp.min(jnp.where(p == p1, lane, LANES), axis=1, keepdims=True)
    p_rest = jnp.where(lane == l1, -1.0, p)
    p2 = jnp.max(p_rest, axis=1, keepdims=True)
    l2 = jnp.min(jnp.where(p_rest == p2, lane, LANES), axis=1, keepdims=True)
    denom = p1 + p2
    w1 = p1 / denom * p_group
    w2 = p2 / denom * p_group

    hit1 = lane == l1
    hit2 = lane == l2
    onehot = jnp.where(hit1 | hit2, 1.0, 0.0)
    earlier = lax.broadcasted_iota(I32, (tm, tm), 0) > lax.broadcasted_iota(I32, (tm, tm), 1)
    before = _dot(jnp.where(earlier, 1.0, 0.0).astype(BF16), onehot.astype(BF16)) + run_ref[...]
    r1 = jnp.sum(jnp.where(hit1, before, 0.0), axis=1, keepdims=True)
    r2 = jnp.sum(jnp.where(hit2, before, 0.0), axis=1, keepdims=True)
    run_ref[...] += jnp.sum(onehot, axis=0, keepdims=True)
    cnt_ref[...] = run_ref[...].astype(I32)

    meta = jnp.where(lane == 0, l1 - n_groups,
                     jnp.where(lane == 1, l2 - n_groups, jnp.where(lane == 2, r1, r2)))
    meta_ref[...] = meta.astype(I32)
    wts_ref[...] = jnp.where(lane == 0, w1, w2)


def _moe_route(tokens, n_tok, mods, layer, nw, wr, br, *, seq, n_groups, per_group):
    d = tokens.shape[1]
    tm = TOKEN_TILE
    s = d // 2 // LANES
    mod_row = _mod_row_fn(mods, seq, tm)
    kern = functools.partial(_route_kernel, n_groups=n_groups, per_group=per_group)
    return pl.pallas_call(
        kern,
        grid=(n_tok // tm,),
        in_specs=[
            pl.BlockSpec((tm, d), lambda i: (i, 0)),
            pl.BlockSpec((None, None, 6, d), lambda i: (layer, mod_row(i), 0, 0)),
            pl.BlockSpec((1, d), lambda i: (0, 0)),
            pl.BlockSpec((d, LANES), lambda i: (0, 0)),
            pl.BlockSpec((1, LANES), lambda i: (0, 0)),
        ],
        out_specs=[
            pl.BlockSpec((tm * s, LANES), lambda i: (i, 0)),
            pl.BlockSpec((tm, LANES), lambda i: (i, 0)),
            pl.BlockSpec((tm, LANES), lambda i: (i, 0)),
            pl.BlockSpec((1, LANES), lambda i: (0, 0)),
        ],
        out_shape=[
            jax.ShapeDtypeStruct((n_tok * s, LANES), U32),
            jax.ShapeDtypeStruct((n_tok, LANES), I32),
            jax.ShapeDtypeStruct((n_tok, LANES), F32),
            jax.ShapeDtypeStruct((1, LANES), I32),
        ],
        scratch_shapes=[pltpu.VMEM((1, LANES), F32)],
        compiler_params=_params("arbitrary"),
        name="moe_route",
    )(tokens, mods, nw, wr, br)


def _route_plan(meta, counts, *, n_tok, n_exp, n_groups, tg, tp):
    ids = meta[:, 0:TOP_K]
    rank = meta[:, TOP_K:2 * TOP_K]
    cnt = counts[0, n_groups:n_groups + n_exp]
    ends = jnp.cumsum(cnt)
    offs = ends - cnt
    pos = offs[ids] + rank
    pos_blocks = pos.T.reshape(TOP_K, n_tok // tp, tp).transpose(1, 0, 2)
    n_row_tiles = TOP_K * n_tok // tg
    first_tile = offs // tg
    n_it = jnp.where(cnt > 0, (ends - 1) // tg - first_tile + 1, 0)
    it_end = jnp.cumsum(n_it)
    it_start = it_end - n_it
    n_items = it_end[-1]
    w = jnp.minimum(jnp.arange(n_row_tiles + n_exp - 1, dtype=I32), n_items - 1)
    item_exp = jnp.sum(it_end[None, :] <= w[:, None], axis=1).astype(I32)
    item_tile = (first_tile[item_exp] + w - it_start[item_exp]).astype(I32)
    row_lo = jnp.maximum(offs[item_exp] - item_tile * tg, 0).astype(I32)
    row_hi = jnp.minimum(ends[item_exp] - item_tile * tg, tg).astype(I32)
    first = jnp.concatenate([jnp.ones((1,), I32), (item_tile[1:] != item_tile[:-1]).astype(I32)])
    return pos_blocks.astype(I32), (item_exp, item_tile, row_lo, row_hi, first, n_items.astype(I32).reshape(1))


PERM_SLOTS = 3


def _token_copy(src, src_tok, dst, dst_tok, s, sem):
    return pltpu.make_async_copy(src.at[pl.ds(pl.multiple_of(src_tok * s, s), s), :],
                                 dst.at[pl.ds(pl.multiple_of(dst_tok * s, s), s), :], sem)


def _sort_kernel(pos_ref, src_hbm, dst_hbm, buf, lsem, rsem, *, s):
    i = pl.program_id(0)
    n = pl.num_programs(0)
    tp = pos_ref.shape[2]
    rows = tp * s
    slot = i % PERM_SLOTS

    def tile_load(step, sl):
        return pltpu.make_async_copy(src_hbm.at[pl.ds(pl.multiple_of(step * rows, rows), rows), :], buf.at[sl],
                                     lsem.at[sl])

    def wait_rows(sl):
        for _ in range(TOP_K):
            pltpu.make_async_copy(buf.at[sl], dst_hbm.at[pl.ds(0, rows), :], rsem.at[sl]).wait()

    @pl.when(i == 0)
    def _():
        tile_load(0, 0).start()

    @pl.when(i + 1 < n)
    def _():
        tile_load(i + 1, (i + 1) % PERM_SLOTS).start()

    tile_load(i, slot).wait()

    def body(r, c):
        for k in range(TOP_K):
            _token_copy(buf.at[slot], r, dst_hbm, pos_ref[0, k, r], s, rsem.at[slot]).start()
        return c
    lax.fori_loop(0, tp, body, 0, unroll=8)

    @pl.when(i >= 1)
    def _():
        wait_rows((i - 1) % PERM_SLOTS)

    @pl.when(i == n - 1)
    def _():
        wait_rows(slot)


def _expert_rows_for_tile(i, n, pos_ref, posn_ref, ys_hbm, ybuf, sem, s):
    slot = i % 2

    def request(p_ref, sl):
        tp = p_ref.shape[2]

        def body(r, c):
            for k in range(TOP_K):
                _token_copy(ys_hbm, p_ref[0, k, r], ybuf.at[sl, k], r, s, sem.at[sl]).start()
            return c
        lax.fori_loop(0, tp, body, 0, unroll=8)

    @pl.when(i == 0)
    def _():
        request(pos_ref, 0)

    @pl.when(i + 1 < n)
    def _():
        request(posn_ref, 1 - slot)

    for k in range(TOP_K):
        pltpu.make_async_copy(ys_hbm.at[pl.ds(0, ybuf.shape[2]), :], ybuf.at[slot, k], sem.at[slot]).wait()
    return slot


def _pos_specs(n_blocks, tp, grid_rank):
    if grid_rank == 1:
        cur, nxt = (lambda i: (i, 0, 0)), (lambda i: (jnp.minimum(i + 1, n_blocks - 1), 0, 0))
    else:
        cur, nxt = (lambda i, j: (i, 0, 0)), (lambda i, j: (jnp.minimum(i + 1, n_blocks - 1), 0, 0))
    return [pl.BlockSpec((1, TOP_K, tp), cur, memory_space=pltpu.SMEM),
            pl.BlockSpec((1, TOP_K, tp), nxt, memory_space=pltpu.SMEM)]


def _sort_rows(pos_blocks, src, *, n_tok, s):
    n_blocks, _, tp = pos_blocks.shape
    return pl.pallas_call(
        functools.partial(_sort_kernel, s=s),
        grid=(n_blocks,),
        in_specs=[pl.BlockSpec((1, TOP_K, tp), lambda i: (i, 0, 0), memory_space=pltpu.SMEM),
                  pl.BlockSpec(memory_space=pl.ANY)],
        out_specs=pl.BlockSpec(memory_space=pl.ANY),
        out_shape=jax.ShapeDtypeStruct((TOP_K * n_tok * s, LANES), src.dtype),
        scratch_shapes=[pltpu.VMEM((PERM_SLOTS, tp * s, LANES), src.dtype),
                        pltpu.SemaphoreType.DMA((PERM_SLOTS,)), pltpu.SemaphoreType.DMA((PERM_SLOTS,))],
        compiler_params=_params("arbitrary"),
        name="moe_sort_rows",
    )(pos_blocks, src)


def _gmm_kernel(ie_ref, it_ref, lo_ref, hi_ref, first_ref, n_ref, xs_ref, wg_ref, wu_ref, wd_ref, ys_ref,
                wg_b, wu_b, wd_b):
    w = pl.program_id(0)

    @pl.when(w < n_ref[0])
    def _():
        @pl.when((w == 0) | (ie_ref[w] != ie_ref[jnp.maximum(w - 1, 0)]))
        def _():
            wg_b[...] = wg_ref[...].astype(BF16)
            wu_b[...] = wu_ref[...].astype(BF16)
            wd_b[...] = wd_ref[...].astype(BF16)

        half = wg_b.shape[0] // 2
        s = half // LANES
        x_lo, x_hi = _unpack_pair(_load_token_tiles(xs_ref, s))
        x_lo, x_hi = x_lo.astype(BF16), x_hi.astype(BF16)
        hg = _dot(x_lo, wg_b[0:half, :]) + _dot(x_hi, wg_b[half:, :])
        hu = _dot(x_lo, wu_b[0:half, :]) + _dot(x_hi, wu_b[half:, :])
        yp = _pack_row(_dot((_silu(hg) * hu).astype(BF16), wd_b[...]))
        rows = lax.broadcasted_iota(I32, yp.shape, 0)
        mine = (rows >= lo_ref[w]) & (rows < hi_ref[w])

        @pl.when(first_ref[w] == 1)
        def _():
            _store_token_tiles(ys_ref, jnp.where(mine, yp, jnp.uint32(0)))

        @pl.when(first_ref[w] == 0)
        def _():
            _store_token_tiles(ys_ref, jnp.where(mine, yp, _load_token_tiles(ys_ref, s)))


def _moe_experts(xs, plan, layer, wg, wu, wd):
    _, _, d, de = wg.shape
    s = d // 2 // LANES
    tg = GROUP_TILE
    n_items_max = plan[0].shape[0]
    grid_spec = pltpu.PrefetchScalarGridSpec(
        num_scalar_prefetch=6,
        grid=(n_items_max,),
        in_specs=[
            pl.BlockSpec((tg * s, LANES), lambda w, ie, it, lo, hi, fi, n: (it[w], 0)),
            pl.BlockSpec((None, None, d, de), lambda w, ie, it, lo, hi, fi, n: (layer, ie[w], 0, 0)),
            pl.BlockSpec((None, None, d, de), lambda w, ie, it, lo, hi, fi, n: (layer, ie[w], 0, 0)),
            pl.BlockSpec((None, None, de, d), lambda w, ie, it, lo, hi, fi, n: (layer, ie[w], 0, 0)),
        ],
        out_specs=pl.BlockSpec((tg * s, LANES), lambda w, ie, it, lo, hi, fi, n: (it[w], 0)),
        scratch_shapes=[pltpu.VMEM((d, de), BF16), pltpu.VMEM((d, de), BF16), pltpu.VMEM((de, d), BF16)],
    )
    return pl.pallas_call(
        _gmm_kernel,
        grid_spec=grid_spec,
        out_shape=jax.ShapeDtypeStruct(xs.shape, U32),
        compiler_params=_params("arbitrary"),
        name="moe_experts",
    )(*plan, xs, wg, wu, wd)


def _moe_combine(y0_ref, y1_ref, wts_ref, s):
    wts = wts_ref[...]
    return (wts[:, 0:1] * _unpack_row(_load_token_tiles(y0_ref, s))
            + wts[:, 1:2] * _unpack_row(_load_token_tiles(y1_ref, s)))


def _inproj_kernel(pos_ref, posn_ref, x_ref, ys_hbm, wts_ref, mp_ref, m_ref, nw_ref, w_ref, wdt_ref, cw_ref, cb_ref,
                   dtb_ref, lat_ref, z_ref, xbc_ref, dt_ref, h_ref, mask_ref, ybuf, ysem, *, nz, n_lat_tiles, ctx_row):
    i = pl.program_id(0)
    j = pl.program_id(1)
    n_sub = w_ref.shape[1] // COL_TILE

    @pl.when(j == 0)
    def _():
        s = x_ref.shape[1] // 2 // LANES
        slot = _expert_rows_for_tile(i, pl.num_programs(0), pos_ref, posn_ref, ys_hbm, ybuf, ysem, s)
        lat = x_ref[...] + mp_ref[5:6, :] * _moe_combine(ybuf.at[slot, 0], ybuf.at[slot, 1], wts_ref, s)
        lat_ref[...] = lat
        h_ref[...] = _mod_norm(lat, nw_ref[...], m_ref[0:1, :], m_ref[1:2, :]).astype(BF16)
        _conv_masks(mask_ref, jnp.where(i < n_lat_tiles, GRID_W, ctx_row))

    @pl.when(j < nz)
    def _():
        for s in range(n_sub):
            sl = slice(s * COL_TILE, (s + 1) * COL_TILE)
            z_ref[:, sl] = _dot(h_ref[...], w_ref[:, sl]).astype(BF16)

    @pl.when(j >= nz)
    def _():
        for s in range(n_sub):
            sl = slice(s * COL_TILE, (s + 1) * COL_TILE)
            r = _dot(h_ref[...], w_ref[:, sl])
            xbc_ref[:, sl] = _silu(_dwconv_rows(r, cw_ref, mask_ref, s * COL_TILE) + cb_ref[:, sl]).astype(BF16)

    @pl.when(j == pl.num_programs(1) - 1)
    def _():
        v = _dot(h_ref[...], wdt_ref[...]) + dtb_ref[...]
        dt_ref[...] = jnp.maximum(v, 0.0) + jnp.log1p(jnp.exp(-jnp.abs(v)))


def _ssd_in_proj(tokens, ys, pos_blocks, wts, mods, layer, nw, w_zx, w_dt, conv_w, conv_b, dt_bias,
                 *, seq, n_lat_tiles, ctx_row, d_inner, conv_dim):
    t, d = tokens.shape
    tm = TOKEN_TILE
    s = d // 2 // LANES
    assert pos_blocks.shape == (t // tm, TOP_K, tm)
    tc = next(c for c in (4 * COL_TILE, 2 * COL_TILE, COL_TILE) if d_inner % c == 0 and conv_dim % c == 0)
    nz, nx = d_inner // tc, conv_dim // tc
    mod_row = _mod_row_fn(mods, seq, tm)

    def xcol(j):
        return jnp.maximum(j - nz, 0)

    kern = functools.partial(_inproj_kernel, nz=nz, n_lat_tiles=n_lat_tiles, ctx_row=ctx_row)
    return pl.pallas_call(
        kern,
        grid=(t // tm, nz + nx),
        in_specs=_pos_specs(t // tm, tm, 2) + [
            pl.BlockSpec((tm, d), lambda i, j: (i, 0)),
            pl.BlockSpec(memory_space=pl.ANY),
            pl.BlockSpec((tm, LANES), lambda i, j: (i, 0)),
            pl.BlockSpec((None, None, 6, d), lambda i, j: (layer - 1, mod_row(i), 0, 0)),
            pl.BlockSpec((None, None, 6, d), lambda i, j: (layer, mod_row(i), 0, 0)),
            pl.BlockSpec((1, d), lambda i, j: (0, 0)),
            pl.BlockSpec((d, tc), lambda i, j: (0, j)),
            pl.BlockSpec((d, LANES), lambda i, j: (0, 0)),
            pl.BlockSpec((conv_w.shape[0], tc), lambda i, j: (0, xcol(j))),
            pl.BlockSpec((1, tc), lambda i, j: (0, xcol(j))),
            pl.BlockSpec((1, LANES), lambda i, j: (0, 0)),
        ],
        out_specs=[
            pl.BlockSpec((tm, d), lambda i, j: (i, 0)),
            pl.BlockSpec((tm, tc), lambda i, j: (i, jnp.minimum(j, nz - 1))),
            pl.BlockSpec((tm, tc), lambda i, j: (i, xcol(j))),
            pl.BlockSpec((tm, LANES), lambda i, j: (i, 0)),
        ],
        out_shape=[
            jax.ShapeDtypeStruct((t, d), F32),
            jax.ShapeDtypeStruct((t, d_inner), BF16),
            jax.ShapeDtypeStruct((t, conv_dim), BF16),
            jax.ShapeDtypeStruct((t, LANES), F32),
        ],
        scratch_shapes=[pltpu.VMEM((tm, d), BF16), pltpu.VMEM((conv_w.shape[0] - 1, tm, LANES), F32),
                        pltpu.VMEM((2, TOP_K, tm * s, LANES), U32), pltpu.SemaphoreType.DMA((2,))],
        compiler_params=_params("arbitrary", "arbitrary"),
        name="ssd_in_proj",
    )(pos_blocks, pos_blocks, tokens, ys, wts, mods, mods, nw, w_zx, w_dt, conv_w, conv_b, dt_bias)


def _ssd_kernel(x_ref, b_ref, c_ref, dt_ref, a_ref, y_ref, st_ref, *, n_heads):
    d = pl.program_id(1)
    k = pl.program_id(2)
    q = x_ref.shape[0]
    n_groups = st_ref.shape[0]
    gw = st_ref.shape[2]
    hpg = gw // HEAD_DIM
    pw = 2 * HEAD_DIM
    fwd = d == 0

    @pl.when(k == 0)
    def _():
        st_ref[...] = jnp.zeros_like(st_ref)

    t_i = lax.broadcasted_iota(I32, (q, q), 0)
    s_i = lax.broadcasted_iota(I32, (q, q), 1)
    incl = jnp.where(fwd, t_i - s_i, s_i - t_i) >= 0

    dt_all = dt_ref[...]
    dta_all = dt_all * (a_ref[...] * LOG2_E)
    cum_all = jnp.dot(jnp.where(incl, 1.0, 0.0), dta_all, preferred_element_type=F32,
                      precision=lax.Precision.HIGHEST)
    cum_t_all = cum_all.T
    dt_t_all = dt_all.T
    tot_t_all = jnp.sum(dta_all.T, axis=1, keepdims=True)

    def pick_cols(a):
        return jnp.where(fwd, a[:, 0:n_heads], a[:, n_heads:2 * n_heads])

    def pick_rows(a):
        return jnp.where(fwd, a[0:n_heads, :], a[n_heads:2 * n_heads, :])

    cum = pick_cols(cum_all)
    tot = pick_cols(jnp.sum(dta_all, axis=0, keepdims=True))
    cum_t = pick_rows(cum_t_all)
    dt_t = pick_rows(dt_t_all)
    to_end_t = (dt_t * jnp.exp2(pick_rows(tot_t_all) - cum_t)).astype(BF16)
    dt_tb = dt_t.astype(BF16)
    dec_all = jnp.exp2(tot)

    low_half = lax.broadcasted_iota(I32, (q, pw), 1) < HEAD_DIM
    low_row = lax.broadcasted_iota(I32, (1, pw), 1) < HEAD_DIM

    for g in range(n_groups):
        bg = b_ref[:, g * D_STATE:(g + 1) * D_STATE]
        cg = c_ref[:, g * D_STATE:(g + 1) * D_STATE]
        cb = lax.dot_general(cg, bg, (((1,), (1,)), ((), ())), preferred_element_type=F32)
        cb = jnp.where(incl, cb, 0.0).astype(BF16)
        bg_t = bg.astype(F32).T.astype(BF16)
        st = st_ref[g]
        y_off = _dot(cg, st.astype(BF16))

        for pair in range(hpg // 2):
            psl = slice(pair * pw, (pair + 1) * pw)
            xp = x_ref[:, g * gw + pair * pw:g * gw + (pair + 1) * pw]
            zero = jnp.zeros_like(xp)
            rhs = jnp.concatenate([jnp.where(low_half, xp, zero), jnp.where(low_half, zero, xp)], axis=0)
            w_parts, u_parts, cols = [], [], []
            for hh in range(2):
                h = g * hpg + pair * 2 + hh
                col = jnp.broadcast_to(cum[:, h:h + 1], (q, q))
                seg = jnp.exp2(jnp.minimum(col - cum_t[h:h + 1, :], 0.0))
                w_parts.append(cb * seg.astype(BF16) * dt_tb[h:h + 1, :])
                u_parts.append(bg_t * to_end_t[h:h + 1, :])
                cols.append(col)
            y_diag = _dot(jnp.concatenate(w_parts, axis=1), rhs)
            upd = _dot(jnp.concatenate(u_parts, axis=1), rhs)
            dec_in = jnp.exp2(jnp.where(low_half, cols[0], cols[1]))
            y_ref[:, g * gw + pair * pw:g * gw + (pair + 1) * pw] = (y_diag + y_off[:, psl] * dec_in).astype(BF16)
            h0 = g * hpg + pair * 2
            dec_pair = jnp.where(low_row, jnp.broadcast_to(dec_all[:, h0:h0 + 1], (1, pw)),
                                 jnp.broadcast_to(dec_all[:, h0 + 1:h0 + 2], (1, pw)))
            st_ref[g, :, psl] = st[:, psl] * dec_pair + upd


def _ssd_scan(xbc, dt, a_neg, *, n_batch, lat_chunks, ctx_chunks, d_inner, n_heads):
    t = xbc.shape[0]
    q = CHUNK
    gn = SSD_GROUPS * D_STATE
    gw = d_inner // SSD_GROUPS
    assert gw % (2 * HEAD_DIM) == 0 and d_inner % gn == 0 and q == 2 * HEAD_DIM and q == LANES
    n_steps = ctx_chunks + lat_chunks
    ctx_base = n_batch * lat_chunks

    def chunk(b, d, k):
        kc = jnp.where(d == 0, k, ctx_chunks - 1 - k)
        kl = jnp.where(d == 0, k - ctx_chunks, n_steps - 1 - k)
        return jnp.where(k < ctx_chunks, ctx_base + b * ctx_chunks + kc, b * lat_chunks + kl)

    kern = functools.partial(_ssd_kernel, n_heads=n_heads)
    return pl.pallas_call(
        kern,
        grid=(n_batch, 2, n_steps),
        in_specs=[
            pl.BlockSpec((q, d_inner), lambda b, d, k: (chunk(b, d, k), 0)),
            pl.BlockSpec((q, gn), lambda b, d, k: (chunk(b, d, k), d_inner // gn)),
            pl.BlockSpec((q, gn), lambda b, d, k: (chunk(b, d, k), d_inner // gn + 1)),
            pl.BlockSpec((q, LANES), lambda b, d, k: (chunk(b, d, k), 0)),
            pl.BlockSpec((1, LANES), lambda b, d, k: (0, 0)),
        ],
        out_specs=pl.BlockSpec((None, q, d_inner), lambda b, d, k: (d, chunk(b, d, k), 0)),
        out_shape=jax.ShapeDtypeStruct((2, t, d_inner), BF16),
        scratch_shapes=[pltpu.VMEM((SSD_GROUPS, D_STATE, gw), F32)],
        compiler_params=_params("arbitrary", "arbitrary", "arbitrary"),
        name="ssd_scan",
    )(xbc, xbc, xbc, dt, a_neg)


def _outproj_kernel(yf_ref, yb_ref, xs_ref, z_ref, dsk_ref, gnw_ref, lat_ref, m_ref, w_ref, o_ref):
    gw = w_ref.shape[0] // SSD_GROUPS
    acc = None
    for g in range(SSD_GROUPS):
        sl = slice(g * gw, (g + 1) * gw)
        y = (yf_ref[:, sl].astype(F32) + yb_ref[:, sl].astype(F32)
             + dsk_ref[:, sl] * xs_ref[:, sl].astype(F32))
        gated = y * _silu(z_ref[:, sl].astype(F32))
        ms = jnp.mean(gated * gated, axis=-1, keepdims=True)
        gn = (gated * lax.rsqrt(ms + EPS) * gnw_ref[:, sl]).astype(BF16)
        part = _dot(gn, w_ref[sl, :])
        acc = part if acc is None else acc + part
    o_ref[...] = lat_ref[...] + m_ref[2:3, :] * acc


def _ssd_out_proj(y, xbc, z, d_skip, gnw, tokens, mods, layer, w_out, *, n_tok, seq):
    d_inner, d = w_out.shape
    tm = OUT_TOKEN_TILE
    mod_row = _mod_row_fn(mods, seq, tm)
    return pl.pallas_call(
        _outproj_kernel,
        grid=(n_tok // tm,),
        in_specs=[
            pl.BlockSpec((None, tm, d_inner), lambda i: (0, i, 0)),
            pl.BlockSpec((None, tm, d_inner), lambda i: (1, i, 0)),
            pl.BlockSpec((tm, d_inner), lambda i: (i, 0)),
            pl.BlockSpec((tm, d_inner), lambda i: (i, 0)),
            pl.BlockSpec((1, d_inner), lambda i: (0, 0)),
            pl.BlockSpec((1, d_inner), lambda i: (0, 0)),
            pl.BlockSpec((tm, d), lambda i: (i, 0)),
            pl.BlockSpec((None, None, 6, d), lambda i: (layer, mod_row(i), 0, 0)),
            pl.BlockSpec((d_inner, d), lambda i: (0, 0), pipeline_mode=pl.Buffered(1)),
        ],
        out_specs=pl.BlockSpec((tm, d), lambda i: (i, 0)),
        out_shape=jax.ShapeDtypeStruct((n_tok, d), F32),
        compiler_params=_params("arbitrary"),
        name="ssd_out_proj",
    )(y, y, xbc, z, d_skip, gnw, tokens, mods, w_out)


def _final_kernel(pos_ref, posn_ref, x_ref, ys_hbm, wts_ref, m_ref, nw_ref, o_ref, ybuf, ysem):
    s = x_ref.shape[1] // 2 // LANES
    slot = _expert_rows_for_tile(pl.program_id(0), pl.num_programs(0), pos_ref, posn_ref, ys_hbm, ybuf, ysem, s)
    lat = x_ref[...] + m_ref[5:6, :] * _moe_combine(ybuf.at[slot, 0], ybuf.at[slot, 1], wts_ref, s)
    ms = jnp.mean(lat * lat, axis=-1, keepdims=True)
    o_ref[...] = lat * lax.rsqrt(ms + EPS) * nw_ref[...]


def _final_norm(tokens, ys, pos_blocks, wts, mods, layer, nw, *, seq):
    t, d = tokens.shape
    tm = TOKEN_TILE
    s = d // 2 // LANES
    assert pos_blocks.shape == (t // tm, TOP_K, tm)
    mod_row = _mod_row_fn(mods, seq, tm)
    return pl.pallas_call(
        _final_kernel,
        grid=(t // tm,),
        in_specs=_pos_specs(t // tm, tm, 1) + [
            pl.BlockSpec((tm, d), lambda i: (i, 0)),
            pl.BlockSpec(memory_space=pl.ANY),
            pl.BlockSpec((tm, LANES), lambda i: (i, 0)),
            pl.BlockSpec((None, None, 6, d), lambda i: (layer, mod_row(i), 0, 0)),
            pl.BlockSpec((1, d), lambda i: (0, 0)),
        ],
        out_specs=pl.BlockSpec((tm, d), lambda i: (i, 0)),
        out_shape=jax.ShapeDtypeStruct((t, d), F32),
        scratch_shapes=[pltpu.VMEM((2, TOP_K, tm * s, LANES), U32), pltpu.SemaphoreType.DMA((2,))],
        compiler_params=_params("arbitrary"),
        name="final_norm",
    )(pos_blocks, pos_blocks, tokens, ys, wts, mods, nw)


def _router_weights(rg_w, rg_b, re_w, re_b):
    d, n_groups = rg_w.shape
    n_exp = re_w.shape[1]
    pad = LANES - n_groups - n_exp
    wr = jnp.concatenate([rg_w, re_w, jnp.zeros((d, pad), F32)], axis=1).astype(BF16)
    br = jnp.concatenate([rg_b, re_b, jnp.zeros((pad,), F32)]).reshape(1, LANES)
    return wr, br


def _moe(tokens, n_tok, mods, layer, nw, rg_w, rg_b, re_w, re_b, w_gate, w_up, w_down, *, seq):
    n_groups = rg_w.shape[1]
    n_exp = re_w.shape[1]
    s = tokens.shape[1] // 2 // LANES
    assert n_tok % PERM_TILE == 0 and (TOP_K * n_tok) % GROUP_TILE == 0 and tokens.shape[1] % (2 * LANES) == 0
    assert PERM_TILE == TOKEN_TILE
    wr, br = _router_weights(rg_w, rg_b, re_w, re_b)
    fp, meta, wts, counts = _moe_route(tokens, n_tok, mods, layer, nw, wr, br, seq=seq, n_groups=n_groups,
                                       per_group=n_exp // n_groups)
    pos_blocks, plan = _route_plan(meta, counts, n_tok=n_tok, n_exp=n_exp, n_groups=n_groups,
                                   tg=GROUP_TILE, tp=PERM_TILE)
    xs = _sort_rows(pos_blocks, fp, n_tok=n_tok, s=s)
    return _moe_experts(xs, plan, layer, w_gate, w_up, w_down), pos_blocks, wts


def kernel(x, c, ctx, c_ctx, ada_w, ada_b, norm_mix_w, norm_ffn_w, sc_w_in, sc_conv_w, sc_w_out, ssd_w_in,
           ssd_conv_w, ssd_conv_b, ssd_dt_bias, ssd_a_log, ssd_d, ssd_norm_w, ssd_w_out, rg_w, rg_b, re_w, re_b,
           moe_w_gate, moe_w_up, moe_w_down, final_norm_w):
    n_batch, seq, d = x.shape
    ctx_len = ctx.shape[1]
    depth = ada_w.shape[0]
    d_inner = ssd_w_out.shape[1]
    conv_dim = ssd_conv_w.shape[2]
    n_heads = ssd_d.shape[1]
    tm = TOKEN_TILE
    t_lat, t_ctx = n_batch * seq, n_batch * ctx_len
    assert depth == 2 and seq % tm == 0 and t_ctx % tm == 0 and tm % ctx_len == 0 and tm % GRID_W == 0
    assert ctx_len & (ctx_len - 1) == 0 and GRID_W & (GRID_W - 1) == 0
    assert seq % CHUNK == 0 and ctx_len % CHUNK == 0 and d_inner == n_heads * HEAD_DIM
    assert n_batch + 1 <= SUBLANES and 2 * n_heads <= LANES and d_inner % COL_TILE == 0 and conv_dim % COL_TILE == 0
    n_lat_tiles = t_lat // tm

    cvec = jnp.zeros((SUBLANES, d), F32).at[0:n_batch].set(c).at[n_batch].set(c_ctx)
    mods = _ada_mods(cvec, ada_w, ada_b)

    x_lat = x.reshape(t_lat, d)
    x_ctx = ctx.reshape(t_ctx, d)

    tokens = _short_conv_layer(x_lat, x_ctx, mods, 0, norm_mix_w[0:1], sc_w_in[0].astype(BF16), sc_conv_w[0],
                               sc_w_out[0].astype(BF16), seq=seq, ctx_row=ctx_len)
    t_all = t_lat + t_ctx
    ys, pos_blocks, wts = _moe(tokens, t_all, mods, 0, norm_ffn_w[0:1], rg_w[0], rg_b[0], re_w[0], re_b[0],
                               moe_w_gate, moe_w_up, moe_w_down, seq=seq)

    w_in = ssd_w_in[0]
    dt_cols = 2 * n_heads
    w_zx = w_in[:, :d_inner + conv_dim].astype(BF16)
    w_dt = jnp.concatenate([w_in[:, d_inner + conv_dim:], jnp.zeros((d, LANES - dt_cols), F32)],
                           axis=1).astype(BF16)
    dt_bias = jnp.zeros((1, LANES), F32).at[0, :dt_cols].set(ssd_dt_bias[0].reshape(-1))
    a_neg = jnp.zeros((1, LANES), F32).at[0, :dt_cols].set(-jnp.exp(ssd_a_log[0].reshape(-1)))
    tokens, z, xbc, dt = _ssd_in_proj(tokens, ys, pos_blocks, wts, mods, 1, norm_mix_w[1:2], w_zx, w_dt, ssd_conv_w[0],
                                      ssd_conv_b[0].reshape(1, conv_dim), dt_bias, seq=seq,
                                      n_lat_tiles=n_lat_tiles, ctx_row=ctx_len, d_inner=d_inner, conv_dim=conv_dim)
    y = _ssd_scan(xbc, dt, a_neg, n_batch=n_batch, lat_chunks=seq // CHUNK, ctx_chunks=ctx_len // CHUNK,
                  d_inner=d_inner, n_heads=n_heads)
    d_skip = jnp.repeat(ssd_d[0], HEAD_DIM).reshape(1, d_inner)
    lat = _ssd_out_proj(y, xbc, z, d_skip, ssd_norm_w[0].reshape(1, d_inner), tokens, mods, 1,
                        ssd_w_out[0].astype(BF16), n_tok=t_lat, seq=seq)

    ys, pos_blocks, wts = _moe(lat, t_lat, mods, 1, norm_ffn_w[1:2], rg_w[1], rg_b[1], re_w[1], re_b[1],
                               moe_w_gate, moe_w_up, moe_w_down, seq=seq)
    out = _final_norm(lat, ys, pos_blocks, wts, mods, 1, final_norm_w.reshape(1, d), seq=seq)
    return out.reshape(n_batch, seq, d)
```

```python
import functools

import jax
import jax.numpy as jnp
from jax import lax
from jax.experimental import pallas as pl
from jax.experimental.pallas import tpu as pltpu

F32 = jnp.float32
BF16 = jnp.bfloat16
I32 = jnp.int32
U32 = jnp.uint32

EPS = 1e-6
GRID_W = 64
HEAD_DIM = 64
D_STATE = 128
SSD_GROUPS = 8
CHUNK = 128
TOP_K = 2

LANES = 128
SUBLANES = 8
VMEM_LIMIT = 58 * 1024 * 1024
TOKEN_TILE = 512
COL_TILE = 512
OUT_TOKEN_TILE = 256
GROUP_TILE = 256
PERM_TILE = 512
ADA_COL_TILE = 1024
HI_MASK = 0xFFFF0000
LOG2_E = 1.4426950408889634


def _params(*sem):
    return pltpu.CompilerParams(dimension_semantics=sem, vmem_limit_bytes=VMEM_LIMIT)


def _dot(a, b):
    return jnp.dot(a, b, preferred_element_type=F32)


def _silu(x):
    return x * jax.nn.sigmoid(x)


def _mod_norm(x, nw, shift, scale):
    ms = jnp.mean(x * x, axis=-1, keepdims=True)
    y = x * lax.rsqrt(ms + EPS)
    return (y * nw) * (1.0 + scale) + shift


def _pack_pair(lo, hi):
    lo_bits = lax.bitcast_convert_type(lo.astype(BF16).astype(F32), U32)
    hi_bits = lax.bitcast_convert_type(hi.astype(BF16).astype(F32), U32)
    return (lo_bits >> 16) | (hi_bits & jnp.uint32(HI_MASK))


def _unpack_pair(u):
    lo = lax.bitcast_convert_type(u << 16, F32)
    hi = lax.bitcast_convert_type(u & jnp.uint32(HI_MASK), F32)
    return lo, hi


def _pack_row(x):
    half = x.shape[1] // 2
    return _pack_pair(x[:, :half], x[:, half:])


def _unpack_row(u):
    lo, hi = _unpack_pair(u)
    return jnp.concatenate([lo, hi], axis=1)


def _store_token_tiles(ref, packed):
    rows, width = packed.shape
    s = width // LANES
    for c in range(s):
        ref[pl.ds(c, rows, stride=s), :] = packed[:, c * LANES:(c + 1) * LANES]


def _load_token_tiles(ref, s):
    rows = ref.shape[0] // s
    return jnp.concatenate([ref[pl.ds(c, rows, stride=s), :] for c in range(s)], axis=1)


def _conv_masks(mask_ref, row_len):
    n_side, tm, _ = mask_ref.shape
    half = n_side // 2
    pos = lax.broadcasted_iota(I32, (tm, LANES), 0) & (row_len - 1)
    for m in range(n_side):
        off = m - half if m < half else m - half + 1
        valid = (pos + off >= 0) & (pos + off < row_len)
        mask_ref[m] = jnp.where(valid, 1.0, 0.0)


def _dwconv_rows(u, cw_ref, mask_ref, col0):
    tm, width_c = u.shape
    width = cw_ref.shape[0]
    half = width // 2
    outs = []
    for c in range(width_c // LANES):
        sl = slice(c * LANES, (c + 1) * LANES)
        wsl = slice(col0 + c * LANES, col0 + (c + 1) * LANES)
        uc = u[:, sl]
        out = cw_ref[half:half + 1, wsl] * uc
        for k in range(width):
            off = k - half
            if off == 0:
                continue
            m = k if k < half else k - 1
            shifted = pltpu.roll(uc, (-off) % tm, axis=0)
            out = out + cw_ref[k:k + 1, wsl] * (shifted * mask_ref[m])
        outs.append(out)
    return outs[0] if len(outs) == 1 else jnp.concatenate(outs, axis=1)


def _ada_kernel(c_ref, w_ref, b_ref, o_ref):
    s = _silu(c_ref[...])
    o_ref[...] = jnp.dot(s, w_ref[...], preferred_element_type=F32,
                         precision=lax.Precision.HIGHEST) + b_ref[...]


def _ada_mods(cvec, ada_w, ada_b):
    depth, d, d6 = ada_w.shape
    tn = min(ADA_COL_TILE, d6)
    out = pl.pallas_call(
        _ada_kernel,
        grid=(depth, d6 // tn),
        in_specs=[pl.BlockSpec((SUBLANES, d), lambda l, j: (0, 0)),
                  pl.BlockSpec((None, d, tn), lambda l, j: (l, 0, j)),
                  pl.BlockSpec((None, 1, tn), lambda l, j: (l, 0, j))],
        out_specs=pl.BlockSpec((None, SUBLANES, tn), lambda l, j: (l, 0, j)),
        out_shape=jax.ShapeDtypeStruct((depth, SUBLANES, d6), F32),
        compiler_params=_params("arbitrary", "arbitrary"),
        name="ada_mods",
    )(cvec, ada_w, ada_b.reshape(depth, 1, d6))
    return out.reshape(depth, SUBLANES, 6, d)


def _mod_row_fn(mods, tokens_per_batch, tile):
    n_rows = mods.shape[1]
    per_batch = tokens_per_batch // tile

    def mod_row(i):
        return jnp.minimum(i // per_batch, n_rows - 1)
    return mod_row


def _sc_kernel(xl_ref, xc_ref, m_ref, nw_ref, wb_ref, wc_ref, wv_ref, cw_ref, wo_ref,
               o_ref, h_ref, acc_ref, mask_ref, *, n_lat_tiles, ctx_row):
    i = pl.program_id(0)
    j = pl.program_id(1)
    is_lat = i < n_lat_tiles

    def for_source(fn):
        pl.when(is_lat)(lambda: fn(xl_ref))
        pl.when(jnp.logical_not(is_lat))(lambda: fn(xc_ref))

    @pl.when(j == 0)
    def _():
        def prologue(x_ref):
            h_ref[...] = _mod_norm(x_ref[...], nw_ref[...], m_ref[0:1, :], m_ref[1:2, :]).astype(BF16)
        for_source(prologue)
        acc_ref[...] = jnp.zeros_like(acc_ref)
        _conv_masks(mask_ref, jnp.where(is_lat, GRID_W, ctx_row))

    h = h_ref[...]
    gate_b = _dot(h, wb_ref[...])
    u = _dot(h, wc_ref[...]) * _dot(h, wv_ref[...])
    g = (gate_b * _dwconv_rows(u, cw_ref, mask_ref, 0)).astype(BF16)
    acc_ref[...] += _dot(g, wo_ref[...])

    @pl.when(j == pl.num_programs(1) - 1)
    def _():
        def epilogue(x_ref):
            o_ref[...] = x_ref[...] + m_ref[2:3, :] * acc_ref[...]
        for_source(epilogue)


def _short_conv_layer(x_lat, x_ctx, mods, layer, nw, w_in, conv_w, w_out, *, seq, ctx_row):
    t_lat, d = x_lat.shape
    t_ctx = x_ctx.shape[0]
    tm, tc = TOKEN_TILE, min(COL_TILE, d)
    n_lat, n_ctx, nj = t_lat // tm, t_ctx // tm, d // tc
    mod_row = _mod_row_fn(mods, seq, tm)
    kern = functools.partial(_sc_kernel, n_lat_tiles=n_lat, ctx_row=ctx_row)
    return pl.pallas_call(
        kern,
        grid=(n_lat + n_ctx, nj),
        in_specs=[
            pl.BlockSpec((tm, d), lambda i, j: (jnp.minimum(i, n_lat - 1), 0)),
            pl.BlockSpec((tm, d), lambda i, j: (jnp.maximum(i - n_lat, 0), 0)),
            pl.BlockSpec((None, None, 6, d), lambda i, j: (layer, mod_row(i), 0, 0)),
            pl.BlockSpec((1, d), lambda i, j: (0, 0)),
            pl.BlockSpec((d, tc), lambda i, j: (0, j)),
            pl.BlockSpec((d, tc), lambda i, j: (0, nj + j)),
            pl.BlockSpec((d, tc), lambda i, j: (0, 2 * nj + j)),
            pl.BlockSpec((conv_w.shape[0], tc), lambda i, j: (0, j)),
            pl.BlockSpec((tc, d), lambda i, j: (j, 0)),
        ],
        out_specs=pl.BlockSpec((tm, d), lambda i, j: (i, 0)),
        out_shape=jax.ShapeDtypeStruct((t_lat + t_ctx, d), F32),
        scratch_shapes=[pltpu.VMEM((tm, d), BF16), pltpu.VMEM((tm, d), F32),
                        pltpu.VMEM((conv_w.shape[0] - 1, tm, LANES), F32)],
        compiler_params=_params("arbitrary", "arbitrary"),
        name="short_conv_mixer",
    )(x_lat, x_ctx, mods, nw, w_in, w_in, w_in, conv_w, w_out)


def _route_kernel(x_ref, m_ref, nw_ref, wr_ref, br_ref, fp_ref, meta_ref, wts_ref, cnt_ref, run_ref,
                  *, n_groups, per_group):
    i = pl.program_id(0)

    @pl.when(i == 0)
    def _():
        run_ref[...] = jnp.zeros_like(run_ref)

    f = _mod_norm(x_ref[...], nw_ref[...], m_ref[3:4, :], m_ref[4:5, :])
    _store_token_tiles(fp_ref, _pack_row(f))
    logits = _dot(f.astype(BF16), wr_ref[...]) + br_ref[...]
    tm = logits.shape[0]
    lane = lax.broadcasted_iota(I32, logits.shape, 1).astype(F32)
    neg = -jnp.inf

    g_mask = lane < n_groups
    gl = jnp.where(g_mask, logits, neg)
    g_max = jnp.max(gl, axis=1, keepdims=True)
    g_sel = jnp.min(jnp.where(gl == g_max, lane, LANES), axis=1, keepdims=True)
    p_group = 1.0 / jnp.sum(jnp.where(g_mask, jnp.exp(gl - g_max), 0.0), axis=1, keepdims=True)

    e_lo = n_groups + g_sel * per_group
    in_group = (lane >= e_lo) & (lane < e_lo + per_group)
    el = jnp.where(in_group, logits, neg)
    e_max = jnp.max(el, axis=1, keepdims=True)
    ex = jnp.where(in_group, jnp.exp(el - e_max), 0.0)
    p = jnp.where(in_group, ex / jnp.sum(ex, axis=1, keepdims=True), -1.0)
    p1 = jnp.max(p, axis=1, keepdims=True)
    l1 = jnp.min(jnp.where(p == p1, lane, LANES), axis=1, keepdims=True)
    p_rest = jnp.where(lane == l1, -1.0, p)
    p2 = jnp.max(p_rest, axis=1, keepdims=True)
    l2 = jnp.min(jnp.where(p_rest == p2, lane, LANES), axis=1, keepdims=True)
    denom = p1 + p2
    w1 = p1 / denom * p_group
    w2 = p2 / denom * p_group

    hit1 = lane == l1
    hit2 = lane == l2
    onehot = jnp.where(hit1 | hit2, 1.0, 0.0)
    earlier = lax.broadcasted_iota(I32, (tm, tm), 0) > lax.broadcasted_iota(I32, (tm, tm), 1)
    before = _dot(jnp.where(earlier, 1.0, 0.0).astype(BF16), onehot.astype(BF16)) + run_ref[...]
    r1 = jnp.sum(jnp.where(hit1, before, 0.0), axis=1, keepdims=True)
    r2 = jnp.sum(jnp.where(hit2, before, 0.0), axis=1, keepdims=True)
    run_ref[...] += jnp.sum(onehot, axis=0, keepdims=True)
    cnt_ref[...] = run_ref[...].astype(I32)

    meta = jnp.where(lane == 0, l1 - n_groups,
                     jnp.where(lane == 1, l2 - n_groups, jnp.where(lane == 2, r1, r2)))
    meta_ref[...] = meta.astype(I32)
    wts_ref[...] = jnp.where(lane == 0, w1, w2)


def _moe_route(tokens, n_tok, mods, layer, nw, wr, br, *, seq, n_groups, per_group):
    d = tokens.shape[1]
    tm = TOKEN_TILE
    s = d // 2 // LANES
    mod_row = _mod_row_fn(mods, seq, tm)
    kern = functools.partial(_route_kernel, n_groups=n_groups, per_group=per_group)
    return pl.pallas_call(
        kern,
        grid=(n_tok // tm,),
        in_specs=[
            pl.BlockSpec((tm, d), lambda i: (i, 0)),
            pl.BlockSpec((None, None, 6, d), lambda i: (layer, mod_row(i), 0, 0)),
            pl.BlockSpec((1, d), lambda i: (0, 0)),
            pl.BlockSpec((d, LANES), lambda i: (0, 0)),
            pl.BlockSpec((1, LANES), lambda i: (0, 0)),
        ],
        out_specs=[
            pl.BlockSpec((tm * s, LANES), lambda i: (i, 0)),
            pl.BlockSpec((tm, LANES), lambda i: (i, 0)),
            pl.BlockSpec((tm, LANES), lambda i: (i, 0)),
            pl.BlockSpec((1, LANES), lambda i: (0, 0)),
        ],
        out_shape=[
            jax.ShapeDtypeStruct((n_tok * s, LANES), U32),
            jax.ShapeDtypeStruct((n_tok, LANES), I32),
            jax.ShapeDtypeStruct((n_tok, LANES), F32),
            jax.ShapeDtypeStruct((1, LANES), I32),
        ],
        scratch_shapes=[pltpu.VMEM((1, LANES), F32)],
        compiler_params=_params("arbitrary"),
        name="moe_route",
    )(tokens, mods, nw, wr, br)


def _route_plan(meta, counts, *, n_tok, n_exp, n_groups, tg, tp):
    ids = meta[:, 0:TOP_K]
    rank = meta[:, TOP_K:2 * TOP_K]
    cnt = counts[0, n_groups:n_groups + n_exp]
    ends = jnp.cumsum(cnt)
    offs = ends - cnt
    pos = offs[ids] + rank
    pos_blocks = pos.T.reshape(TOP_K, n_tok // tp, tp).transpose(1, 0, 2)
    n_row_tiles = TOP_K * n_tok // tg
    first_tile = offs // tg
    n_it = jnp.where(cnt > 0, (ends - 1) // tg - first_tile + 1, 0)
    it_end = jnp.cumsum(n_it)
    it_start = it_end - n_it
    n_items = it_end[-1]
    w = jnp.minimum(jnp.arange(n_row_tiles + n_exp - 1, dtype=I32), n_items - 1)
    item_exp = jnp.sum(it_end[None, :] <= w[:, None], axis=1).astype(I32)
    item_tile = (first_tile[item_exp] + w - it_start[item_exp]).astype(I32)
    row_lo = jnp.maximum(offs[item_exp] - item_tile * tg, 0).astype(I32)
    row_hi = jnp.minimum(ends[item_exp] - item_tile * tg, tg).astype(I32)
    first = jnp.concatenate([jnp.ones((1,), I32), (item_tile[1:] != item_tile[:-1]).astype(I32)])
    return pos_blocks.astype(I32), (item_exp, item_tile, row_lo, row_hi, first, n_items.astype(I32).reshape(1))


PERM_SLOTS = 3


def _token_copy(src, src_tok, dst, dst_tok, s, sem):
    return pltpu.make_async_copy(src.at[pl.ds(pl.multiple_of(src_tok * s, s), s), :],
                                 dst.at[pl.ds(pl.multiple_of(dst_tok * s, s), s), :], sem)


def _sort_kernel(pos_ref, src_hbm, dst_hbm, buf, lsem, rsem, *, s):
    i = pl.program_id(0)
    n = pl.num_programs(0)
    tp = pos_ref.shape[2]
    rows = tp * s
    slot = i % PERM_SLOTS

    def tile_load(step, sl):
        return pltpu.make_async_copy(src_hbm.at[pl.ds(pl.multiple_of(step * rows, rows), rows), :], buf.at[sl],
                                     lsem.at[sl])

    def wait_rows(sl):
        for _ in range(TOP_K):
            pltpu.make_async_copy(buf.at[sl], dst_hbm.at[pl.ds(0, rows), :], rsem.at[sl]).wait()

    @pl.when(i == 0)
    def _():
        tile_load(0, 0).start()

    @pl.when(i + 1 < n)
    def _():
        tile_load(i + 1, (i + 1) % PERM_SLOTS).start()

    tile_load(i, slot).wait()

    def body(r, c):
        for k in range(TOP_K):
            _token_copy(buf.at[slot], r, dst_hbm, pos_ref[0, k, r], s, rsem.at[slot]).start()
        return c
    lax.fori_loop(0, tp, body, 0, unroll=8)

    @pl.when(i >= 1)
    def _():
        wait_rows((i - 1) % PERM_SLOTS)

    @pl.when(i == n - 1)
    def _():
        wait_rows(slot)


def _expert_rows_for_tile(i, n, pos_ref, posn_ref, ys_hbm, ybuf, sem, s):
    slot = i % 2

    def request(p_ref, sl):
        tp = p_ref.shape[2]

        def body(r, c):
            for k in range(TOP_K):
                _token_copy(ys_hbm, p_ref[0, k, r], ybuf.at[sl, k], r, s, sem.at[sl]).start()
            return c
        lax.fori_loop(0, tp, body, 0, unroll=8)

    @pl.when(i == 0)
    def _():
        request(pos_ref, 0)

    @pl.when(i + 1 < n)
    def _():
        request(posn_ref, 1 - slot)

    for k in range(TOP_K):
        pltpu.make_async_copy(ys_hbm.at[pl.ds(0, ybuf.shape[2]), :], ybuf.at[slot, k], sem.at[slot]).wait()
    return slot


def _unsort_kernel(pos_ref, src_hbm, dst_hbm, buf, rsem, wsem, *, s, n_blocks):
    i = pl.program_id(0)
    tp = pos_ref.shape[2]
    rows = tp * s
    slot = i % PERM_SLOTS

    def write_back(step, sl, k):
        return pltpu.make_async_copy(buf.at[sl, k], dst_hbm.at[k, pl.ds(pl.multiple_of(step * rows, rows), rows), :],
                                     wsem.at[sl])

    def wait_rows(sl):
        for k in range(TOP_K):
            pltpu.make_async_copy(src_hbm.at[pl.ds(0, rows), :], buf.at[sl, k], rsem.at[sl]).wait()

    def finish(step, sl):
        wait_rows(sl)
        for k in range(TOP_K):
            write_back(step, sl, k).start()

    def wait_write_back(step, sl):
        for k in range(TOP_K):
            write_back(step, sl, k).wait()

    @pl.when(i >= PERM_SLOTS)
    def _():
        wait_write_back(i - PERM_SLOTS, slot)

    def body(r, c):
        for k in range(TOP_K):
            _token_copy(src_hbm, pos_ref[0, k, r], buf.at[slot, k], r, s, rsem.at[slot]).start()
        return c
    lax.fori_loop(0, tp, body, 0, unroll=8)

    @pl.when(i >= 1)
    def _():
        finish(i - 1, (i - 1) % PERM_SLOTS)

    @pl.when(i == n_blocks - 1)
    def _():
        finish(i, slot)
        for back in range(min(PERM_SLOTS, n_blocks)):
            wait_write_back(i - back, (i - back) % PERM_SLOTS)


def _unsort_rows(pos_blocks, src, *, n_tok, s):
    n_blocks, _, tp = pos_blocks.shape
    return pl.pallas_call(
        functools.partial(_unsort_kernel, s=s, n_blocks=n_blocks),
        grid=(n_blocks,),
        in_specs=[pl.BlockSpec((1, TOP_K, tp), lambda i: (i, 0, 0), memory_space=pltpu.SMEM),
                  pl.BlockSpec(memory_space=pl.ANY)],
        out_specs=pl.BlockSpec(memory_space=pl.ANY),
        out_shape=jax.ShapeDtypeStruct((TOP_K, n_tok * s, LANES), src.dtype),
        scratch_shapes=[pltpu.VMEM((PERM_SLOTS, TOP_K, tp * s, LANES), src.dtype),
                        pltpu.SemaphoreType.DMA((PERM_SLOTS,)), pltpu.SemaphoreType.DMA((PERM_SLOTS,))],
        compiler_params=_params("arbitrary"),
        name="moe_unsort_rows",
    )(pos_blocks, src)


def _pos_specs(n_blocks, tp):
    return [pl.BlockSpec((1, TOP_K, tp), lambda i: (i, 0, 0), memory_space=pltpu.SMEM),
            pl.BlockSpec((1, TOP_K, tp), lambda i: (jnp.minimum(i + 1, n_blocks - 1), 0, 0),
                         memory_space=pltpu.SMEM)]


def _sort_rows(pos_blocks, src, *, n_tok, s):
    n_blocks, _, tp = pos_blocks.shape
    return pl.pallas_call(
        functools.partial(_sort_kernel, s=s),
        grid=(n_blocks,),
        in_specs=[pl.BlockSpec((1, TOP_K, tp), lambda i: (i, 0, 0), memory_space=pltpu.SMEM),
                  pl.BlockSpec(memory_space=pl.ANY)],
        out_specs=pl.BlockSpec(memory_space=pl.ANY),
        out_shape=jax.ShapeDtypeStruct((TOP_K * n_tok * s, LANES), src.dtype),
        scratch_shapes=[pltpu.VMEM((PERM_SLOTS, tp * s, LANES), src.dtype),
                        pltpu.SemaphoreType.DMA((PERM_SLOTS,)), pltpu.SemaphoreType.DMA((PERM_SLOTS,))],
        compiler_params=_params("arbitrary"),
        name="moe_sort_rows",
    )(pos_blocks, src)


def _gmm_kernel(ie_ref, it_ref, lo_ref, hi_ref, first_ref, n_ref, xs_ref, wg_ref, wu_ref, wd_ref, ys_ref,
                wg_b, wu_b, wd_b):
    w = pl.program_id(0)

    @pl.when(w < n_ref[0])
    def _():
        @pl.when((w == 0) | (ie_ref[w] != ie_ref[jnp.maximum(w - 1, 0)]))
        def _():
            wg_b[...] = wg_ref[...].astype(BF16)
            wu_b[...] = wu_ref[...].astype(BF16)
            wd_b[...] = wd_ref[...].astype(BF16)

        half = wg_b.shape[0] // 2
        s = half // LANES
        x_lo, x_hi = _unpack_pair(_load_token_tiles(xs_ref, s))
        x_lo, x_hi = x_lo.astype(BF16), x_hi.astype(BF16)
        hg = _dot(x_lo, wg_b[0:half, :]) + _dot(x_hi, wg_b[half:, :])
        hu = _dot(x_lo, wu_b[0:half, :]) + _dot(x_hi, wu_b[half:, :])
        yp = _pack_row(_dot((_silu(hg) * hu).astype(BF16), wd_b[...]))
        rows = lax.broadcasted_iota(I32, yp.shape, 0)
        mine = (rows >= lo_ref[w]) & (rows < hi_ref[w])

        @pl.when(first_ref[w] == 1)
        def _():
            _store_token_tiles(ys_ref, jnp.where(mine, yp, jnp.uint32(0)))

        @pl.when(first_ref[w] == 0)
        def _():
            _store_token_tiles(ys_ref, jnp.where(mine, yp, _load_token_tiles(ys_ref, s)))


def _moe_experts(xs, plan, layer, wg, wu, wd):
    _, _, d, de = wg.shape
    s = d // 2 // LANES
    tg = GROUP_TILE
    n_items_max = plan[0].shape[0]
    grid_spec = pltpu.PrefetchScalarGridSpec(
        num_scalar_prefetch=6,
        grid=(n_items_max,),
        in_specs=[
            pl.BlockSpec((tg * s, LANES), lambda w, ie, it, lo, hi, fi, n: (it[w], 0)),
            pl.BlockSpec((None, None, d, de), lambda w, ie, it, lo, hi, fi, n: (layer, ie[w], 0, 0)),
            pl.BlockSpec((None, None, d, de), lambda w, ie, it, lo, hi, fi, n: (layer, ie[w], 0, 0)),
            pl.BlockSpec((None, None, de, d), lambda w, ie, it, lo, hi, fi, n: (layer, ie[w], 0, 0)),
        ],
        out_specs=pl.BlockSpec((tg * s, LANES), lambda w, ie, it, lo, hi, fi, n: (it[w], 0)),
        scratch_shapes=[pltpu.VMEM((d, de), BF16), pltpu.VMEM((d, de), BF16), pltpu.VMEM((de, d), BF16)],
    )
    return pl.pallas_call(
        _gmm_kernel,
        grid_spec=grid_spec,
        out_shape=jax.ShapeDtypeStruct(xs.shape, U32),
        compiler_params=_params("arbitrary"),
        name="moe_experts",
    )(*plan, xs, wg, wu, wd)


def _moe_combine(y0_ref, y1_ref, wts_ref, s):
    wts = wts_ref[...]
    return (wts[:, 0:1] * _unpack_row(_load_token_tiles(y0_ref, s))
            + wts[:, 1:2] * _unpack_row(_load_token_tiles(y1_ref, s)))


def _inproj_kernel(x_ref, y0_ref, y1_ref, wts_ref, mp_ref, m_ref, nw_ref, w_ref, wdt_ref, cw_ref, cb_ref, dtb_ref,
                   lat_ref, z_ref, xbc_ref, dt_ref, h_ref, mask_ref, *, nz, n_lat_tiles, ctx_row):
    i = pl.program_id(0)
    j = pl.program_id(1)
    n_sub = w_ref.shape[1] // COL_TILE

    @pl.when(j == 0)
    def _():
        s = x_ref.shape[1] // 2 // LANES
        lat = x_ref[...] + mp_ref[5:6, :] * _moe_combine(y0_ref, y1_ref, wts_ref, s)
        lat_ref[...] = lat
        h_ref[...] = _mod_norm(lat, nw_ref[...], m_ref[0:1, :], m_ref[1:2, :]).astype(BF16)
        _conv_masks(mask_ref, jnp.where(i < n_lat_tiles, GRID_W, ctx_row))

    @pl.when(j < nz)
    def _():
        for s in range(n_sub):
            sl = slice(s * COL_TILE, (s + 1) * COL_TILE)
            z_ref[:, sl] = _dot(h_ref[...], w_ref[:, sl]).astype(BF16)

    @pl.when(j >= nz)
    def _():
        for s in range(n_sub):
            sl = slice(s * COL_TILE, (s + 1) * COL_TILE)
            r = _dot(h_ref[...], w_ref[:, sl])
            xbc_ref[:, sl] = _silu(_dwconv_rows(r, cw_ref, mask_ref, s * COL_TILE) + cb_ref[:, sl]).astype(BF16)

    @pl.when(j == pl.num_programs(1) - 1)
    def _():
        v = _dot(h_ref[...], wdt_ref[...]) + dtb_ref[...]
        dt_ref[...] = jnp.maximum(v, 0.0) + jnp.log1p(jnp.exp(-jnp.abs(v)))


def _ssd_in_proj(tokens, y2, wts, mods, layer, nw, w_zx, w_dt, conv_w, conv_b, dt_bias,
                 *, seq, n_lat_tiles, ctx_row, d_inner, conv_dim):
    t, d = tokens.shape
    tm = TOKEN_TILE
    tc = next(c for c in (4 * COL_TILE, 2 * COL_TILE, COL_TILE) if d_inner % c == 0 and conv_dim % c == 0)
    nz, nx = d_inner // tc, conv_dim // tc
    mod_row = _mod_row_fn(mods, seq, tm)

    def xcol(j):
        return jnp.maximum(j - nz, 0)

    kern = functools.partial(_inproj_kernel, nz=nz, n_lat_tiles=n_lat_tiles, ctx_row=ctx_row)
    return pl.pallas_call(
        kern,
        grid=(t // tm, nz + nx),
        in_specs=[
            pl.BlockSpec((tm, d), lambda i, j: (i, 0)),
            pl.BlockSpec((None, tm * (d // 2 // LANES), LANES), lambda i, j: (0, i, 0)),
            pl.BlockSpec((None, tm * (d // 2 // LANES), LANES), lambda i, j: (1, i, 0)),
            pl.BlockSpec((tm, LANES), lambda i, j: (i, 0)),
            pl.BlockSpec((None, None, 6, d), lambda i, j: (layer - 1, mod_row(i), 0, 0)),
            pl.BlockSpec((None, None, 6, d), lambda i, j: (layer, mod_row(i), 0, 0)),
            pl.BlockSpec((1, d), lambda i, j: (0, 0)),
            pl.BlockSpec((d, tc), lambda i, j: (0, j)),
            pl.BlockSpec((d, LANES), lambda i, j: (0, 0)),
            pl.BlockSpec((conv_w.shape[0], tc), lambda i, j: (0, xcol(j))),
            pl.BlockSpec((1, tc), lambda i, j: (0, xcol(j))),
            pl.BlockSpec((1, LANES), lambda i, j: (0, 0)),
        ],
        out_specs=[
            pl.BlockSpec((tm, d), lambda i, j: (i, 0)),
            pl.BlockSpec((tm, tc), lambda i, j: (i, jnp.minimum(j, nz - 1))),
            pl.BlockSpec((tm, tc), lambda i, j: (i, xcol(j))),
            pl.BlockSpec((tm, LANES), lambda i, j: (i, 0)),
        ],
        out_shape=[
            jax.ShapeDtypeStruct((t, d), F32),
            jax.ShapeDtypeStruct((t, d_inner), BF16),
            jax.ShapeDtypeStruct((t, conv_dim), BF16),
            jax.ShapeDtypeStruct((t, LANES), F32),
        ],
        scratch_shapes=[pltpu.VMEM((tm, d), BF16), pltpu.VMEM((conv_w.shape[0] - 1, tm, LANES), F32)],
        compiler_params=_params("arbitrary", "arbitrary"),
        name="ssd_in_proj",
    )(tokens, y2, y2, wts, mods, mods, nw, w_zx, w_dt, conv_w, conv_b, dt_bias)


def _ssd_kernel(x_ref, b_ref, c_ref, dt_ref, a_ref, y_ref, st_ref, *, n_heads):
    d = pl.program_id(1)
    k = pl.program_id(2)
    q = x_ref.shape[0]
    n_groups = st_ref.shape[0]
    gw = st_ref.shape[2]
    hpg = gw // HEAD_DIM
    pw = 2 * HEAD_DIM
    fwd = d == 0

    @pl.when(k == 0)
    def _():
        st_ref[...] = jnp.zeros_like(st_ref)

    t_i = lax.broadcasted_iota(I32, (q, q), 0)
    s_i = lax.broadcasted_iota(I32, (q, q), 1)
    incl = jnp.where(fwd, t_i - s_i, s_i - t_i) >= 0

    dt_all = dt_ref[...]
    dta_all = dt_all * (a_ref[...] * LOG2_E)
    cum_all = jnp.dot(jnp.where(incl, 1.0, 0.0), dta_all, preferred_element_type=F32,
                      precision=lax.Precision.HIGHEST)
    cum_t_all = cum_all.T
    dt_t_all = dt_all.T
    tot_t_all = jnp.sum(dta_all.T, axis=1, keepdims=True)

    def pick_cols(a):
        return jnp.where(fwd, a[:, 0:n_heads], a[:, n_heads:2 * n_heads])

    def pick_rows(a):
        return jnp.where(fwd, a[0:n_heads, :], a[n_heads:2 * n_heads, :])

    cum = pick_cols(cum_all)
    tot = pick_cols(jnp.sum(dta_all, axis=0, keepdims=True))
    cum_t = pick_rows(cum_t_all)
    dt_t = pick_rows(dt_t_all)
    to_end_t = (dt_t * jnp.exp2(pick_rows(tot_t_all) - cum_t)).astype(BF16)
    dt_tb = dt_t.astype(BF16)
    dec_all = jnp.exp2(tot)

    low_half = lax.broadcasted_iota(I32, (q, pw), 1) < HEAD_DIM
    low_row = lax.broadcasted_iota(I32, (1, pw), 1) < HEAD_DIM

    for g in range(n_groups):
        bg = b_ref[:, g * D_STATE:(g + 1) * D_STATE]
        cg = c_ref[:, g * D_STATE:(g + 1) * D_STATE]
        cb = lax.dot_general(cg, bg, (((1,), (1,)), ((), ())), preferred_element_type=F32)
        cb = jnp.where(incl, cb, 0.0).astype(BF16)
        bg_t = bg.astype(F32).T.astype(BF16)
        st = st_ref[g]
        y_off = _dot(cg, st.astype(BF16))

        for pair in range(hpg // 2):
            psl = slice(pair * pw, (pair + 1) * pw)
            xp = x_ref[:, g * gw + pair * pw:g * gw + (pair + 1) * pw]
            zero = jnp.zeros_like(xp)
            rhs = jnp.concatenate([jnp.where(low_half, xp, zero), jnp.where(low_half, zero, xp)], axis=0)
            w_parts, u_parts, cols = [], [], []
            for hh in range(2):
                h = g * hpg + pair * 2 + hh
                col = jnp.broadcast_to(cum[:, h:h + 1], (q, q))
                seg = jnp.exp2(jnp.minimum(col - cum_t[h:h + 1, :], 0.0))
                w_parts.append(cb * seg.astype(BF16) * dt_tb[h:h + 1, :])
                u_parts.append(bg_t * to_end_t[h:h + 1, :])
                cols.append(col)
            y_diag = _dot(jnp.concatenate(w_parts, axis=1), rhs)
            upd = _dot(jnp.concatenate(u_parts, axis=1), rhs)
            dec_in = jnp.exp2(jnp.where(low_half, cols[0], cols[1]))
            y_ref[:, g * gw + pair * pw:g * gw + (pair + 1) * pw] = (y_diag + y_off[:, psl] * dec_in).astype(BF16)
            h0 = g * hpg + pair * 2
            dec_pair = jnp.where(low_row, jnp.broadcast_to(dec_all[:, h0:h0 + 1], (1, pw)),
                                 jnp.broadcast_to(dec_all[:, h0 + 1:h0 + 2], (1, pw)))
            st_ref[g, :, psl] = st[:, psl] * dec_pair + upd


def _ssd_scan(xbc, dt, a_neg, *, n_batch, lat_chunks, ctx_chunks, d_inner, n_heads):
    t = xbc.shape[0]
    q = CHUNK
    gn = SSD_GROUPS * D_STATE
    gw = d_inner // SSD_GROUPS
    assert gw % (2 * HEAD_DIM) == 0 and d_inner % gn == 0 and q == 2 * HEAD_DIM and q == LANES
    n_steps = ctx_chunks + lat_chunks
    ctx_base = n_batch * lat_chunks

    def chunk(b, d, k):
        kc = jnp.where(d == 0, k, ctx_chunks - 1 - k)
        kl = jnp.where(d == 0, k - ctx_chunks, n_steps - 1 - k)
        return jnp.where(k < ctx_chunks, ctx_base + b * ctx_chunks + kc, b * lat_chunks + kl)

    kern = functools.partial(_ssd_kernel, n_heads=n_heads)
    return pl.pallas_call(
        kern,
        grid=(n_batch, 2, n_steps),
        in_specs=[
            pl.BlockSpec((q, d_inner), lambda b, d, k: (chunk(b, d, k), 0)),
            pl.BlockSpec((q, gn), lambda b, d, k: (chunk(b, d, k), d_inner // gn)),
            pl.BlockSpec((q, gn), lambda b, d, k: (chunk(b, d, k), d_inner // gn + 1)),
            pl.BlockSpec((q, LANES), lambda b, d, k: (chunk(b, d, k), 0)),
            pl.BlockSpec((1, LANES), lambda b, d, k: (0, 0)),
        ],
        out_specs=pl.BlockSpec((None, q, d_inner), lambda b, d, k: (d, chunk(b, d, k), 0)),
        out_shape=jax.ShapeDtypeStruct((2, t, d_inner), BF16),
        scratch_shapes=[pltpu.VMEM((SSD_GROUPS, D_STATE, gw), F32)],
        compiler_params=_params("arbitrary", "arbitrary", "arbitrary"),
        name="ssd_scan",
    )(xbc, xbc, xbc, dt, a_neg)


def _outproj_kernel(yf_ref, yb_ref, xs_ref, z_ref, dsk_ref, gnw_ref, lat_ref, m_ref, w_ref, o_ref):
    gw = w_ref.shape[0] // SSD_GROUPS
    acc = None
    for g in range(SSD_GROUPS):
        sl = slice(g * gw, (g + 1) * gw)
        y = (yf_ref[:, sl].astype(F32) + yb_ref[:, sl].astype(F32)
             + dsk_ref[:, sl] * xs_ref[:, sl].astype(F32))
        gated = y * _silu(z_ref[:, sl].astype(F32))
        ms = jnp.mean(gated * gated, axis=-1, keepdims=True)
        gn = (gated * lax.rsqrt(ms + EPS) * gnw_ref[:, sl]).astype(BF16)
        part = _dot(gn, w_ref[sl, :])
        acc = part if acc is None else acc + part
    o_ref[...] = lat_ref[...] + m_ref[2:3, :] * acc


def _ssd_out_proj(y, xbc, z, d_skip, gnw, tokens, mods, layer, w_out, *, n_tok, seq):
    d_inner, d = w_out.shape
    tm = OUT_TOKEN_TILE
    mod_row = _mod_row_fn(mods, seq, tm)
    return pl.pallas_call(
        _outproj_kernel,
        grid=(n_tok // tm,),
        in_specs=[
            pl.BlockSpec((None, tm, d_inner), lambda i: (0, i, 0)),
            pl.BlockSpec((None, tm, d_inner), lambda i: (1, i, 0)),
            pl.BlockSpec((tm, d_inner), lambda i: (i, 0)),
            pl.BlockSpec((tm, d_inner), lambda i: (i, 0)),
            pl.BlockSpec((1, d_inner), lambda i: (0, 0)),
            pl.BlockSpec((1, d_inner), lambda i: (0, 0)),
            pl.BlockSpec((tm, d), lambda i: (i, 0)),
            pl.BlockSpec((None, None, 6, d), lambda i: (layer, mod_row(i), 0, 0)),
            pl.BlockSpec((d_inner, d), lambda i: (0, 0), pipeline_mode=pl.Buffered(1)),
        ],
        out_specs=pl.BlockSpec((tm, d), lambda i: (i, 0)),
        out_shape=jax.ShapeDtypeStruct((n_tok, d), F32),
        compiler_params=_params("arbitrary"),
        name="ssd_out_proj",
    )(y, y, xbc, z, d_skip, gnw, tokens, mods, w_out)


def _final_kernel(pos_ref, posn_ref, x_ref, ys_hbm, wts_ref, m_ref, nw_ref, o_ref, ybuf, ysem):
    s = x_ref.shape[1] // 2 // LANES
    slot = _expert_rows_for_tile(pl.program_id(0), pl.num_programs(0), pos_ref, posn_ref, ys_hbm, ybuf, ysem, s)
    lat = x_ref[...] + m_ref[5:6, :] * _moe_combine(ybuf.at[slot, 0], ybuf.at[slot, 1], wts_ref, s)
    ms = jnp.mean(lat * lat, axis=-1, keepdims=True)
    o_ref[...] = lat * lax.rsqrt(ms + EPS) * nw_ref[...]


def _final_norm(tokens, ys, pos_blocks, wts, mods, layer, nw, *, seq):
    t, d = tokens.shape
    tm = TOKEN_TILE
    s = d // 2 // LANES
    assert pos_blocks.shape == (t // tm, TOP_K, tm)
    mod_row = _mod_row_fn(mods, seq, tm)
    return pl.pallas_call(
        _final_kernel,
        grid=(t // tm,),
        in_specs=_pos_specs(t // tm, tm) + [
            pl.BlockSpec((tm, d), lambda i: (i, 0)),
            pl.BlockSpec(memory_space=pl.ANY),
            pl.BlockSpec((tm, LANES), lambda i: (i, 0)),
            pl.BlockSpec((None, None, 6, d), lambda i: (layer, mod_row(i), 0, 0)),
            pl.BlockSpec((1, d), lambda i: (0, 0)),
        ],
        out_specs=pl.BlockSpec((tm, d), lambda i: (i, 0)),
        out_shape=jax.ShapeDtypeStruct((t, d), F32),
        scratch_shapes=[pltpu.VMEM((2, TOP_K, tm * s, LANES), U32), pltpu.SemaphoreType.DMA((2,))],
        compiler_params=_params("arbitrary"),
        name="final_norm",
    )(pos_blocks, pos_blocks, tokens, ys, wts, mods, nw)


def _router_weights(rg_w, rg_b, re_w, re_b):
    d, n_groups = rg_w.shape
    n_exp = re_w.shape[1]
    pad = LANES - n_groups - n_exp
    wr = jnp.concatenate([rg_w, re_w, jnp.zeros((d, pad), F32)], axis=1).astype(BF16)
    br = jnp.concatenate([rg_b, re_b, jnp.zeros((pad,), F32)]).reshape(1, LANES)
    return wr, br


def _moe(tokens, n_tok, mods, layer, nw, rg_w, rg_b, re_w, re_b, w_gate, w_up, w_down, *, seq):
    n_groups = rg_w.shape[1]
    n_exp = re_w.shape[1]
    s = tokens.shape[1] // 2 // LANES
    assert n_tok % PERM_TILE == 0 and (TOP_K * n_tok) % GROUP_TILE == 0 and tokens.shape[1] % (2 * LANES) == 0
    assert PERM_TILE == TOKEN_TILE
    wr, br = _router_weights(rg_w, rg_b, re_w, re_b)
    fp, meta, wts, counts = _moe_route(tokens, n_tok, mods, layer, nw, wr, br, seq=seq, n_groups=n_groups,
                                       per_group=n_exp // n_groups)
    pos_blocks, plan = _route_plan(meta, counts, n_tok=n_tok, n_exp=n_exp, n_groups=n_groups,
                                   tg=GROUP_TILE, tp=PERM_TILE)
    xs = _sort_rows(pos_blocks, fp, n_tok=n_tok, s=s)
    return _moe_experts(xs, plan, layer, w_gate, w_up, w_down), pos_blocks, wts


def kernel(x, c, ctx, c_ctx, ada_w, ada_b, norm_mix_w, norm_ffn_w, sc_w_in, sc_conv_w, sc_w_out, ssd_w_in,
           ssd_conv_w, ssd_conv_b, ssd_dt_bias, ssd_a_log, ssd_d, ssd_norm_w, ssd_w_out, rg_w, rg_b, re_w, re_b,
           moe_w_gate, moe_w_up, moe_w_down, final_norm_w):
    n_batch, seq, d = x.shape
    ctx_len = ctx.shape[1]
    depth = ada_w.shape[0]
    d_inner = ssd_w_out.shape[1]
    conv_dim = ssd_conv_w.shape[2]
    n_heads = ssd_d.shape[1]
    tm = TOKEN_TILE
    t_lat, t_ctx = n_batch * seq, n_batch * ctx_len
    assert depth == 2 and seq % tm == 0 and t_ctx % tm == 0 and tm % ctx_len == 0 and tm % GRID_W == 0
    assert ctx_len & (ctx_len - 1) == 0 and GRID_W & (GRID_W - 1) == 0
    assert seq % CHUNK == 0 and ctx_len % CHUNK == 0 and d_inner == n_heads * HEAD_DIM
    assert n_batch + 1 <= SUBLANES and 2 * n_heads <= LANES and d_inner % COL_TILE == 0 and conv_dim % COL_TILE == 0
    n_lat_tiles = t_lat // tm

    cvec = jnp.zeros((SUBLANES, d), F32).at[0:n_batch].set(c).at[n_batch].set(c_ctx)
    mods = _ada_mods(cvec, ada_w, ada_b)

    x_lat = x.reshape(t_lat, d)
    x_ctx = ctx.reshape(t_ctx, d)

    tokens = _short_conv_layer(x_lat, x_ctx, mods, 0, norm_mix_w[0:1], sc_w_in[0].astype(BF16), sc_conv_w[0],
                               sc_w_out[0].astype(BF16), seq=seq, ctx_row=ctx_len)
    t_all = t_lat + t_ctx
    ys, pos_blocks, wts = _moe(tokens, t_all, mods, 0, norm_ffn_w[0:1], rg_w[0], rg_b[0], re_w[0], re_b[0],
                               moe_w_gate, moe_w_up, moe_w_down, seq=seq)

    w_in = ssd_w_in[0]
    dt_cols = 2 * n_heads
    w_zx = w_in[:, :d_inner + conv_dim].astype(BF16)
    w_dt = jnp.concatenate([w_in[:, d_inner + conv_dim:], jnp.zeros((d, LANES - dt_cols), F32)],
                           axis=1).astype(BF16)
    dt_bias = jnp.zeros((1, LANES), F32).at[0, :dt_cols].set(ssd_dt_bias[0].reshape(-1))
    a_neg = jnp.zeros((1, LANES), F32).at[0, :dt_cols].set(-jnp.exp(ssd_a_log[0].reshape(-1)))
    y2 = _unsort_rows(pos_blocks, ys, n_tok=t_all, s=d // 2 // LANES)
    tokens, z, xbc, dt = _ssd_in_proj(tokens, y2, wts, mods, 1, norm_mix_w[1:2], w_zx, w_dt, ssd_conv_w[0],
                                      ssd_conv_b[0].reshape(1, conv_dim), dt_bias, seq=seq,
                                      n_lat_tiles=n_lat_tiles, ctx_row=ctx_len, d_inner=d_inner, conv_dim=conv_dim)
    y = _ssd_scan(xbc, dt, a_neg, n_batch=n_batch, lat_chunks=seq // CHUNK, ctx_chunks=ctx_len // CHUNK,
                  d_inner=d_inner, n_heads=n_heads)
    d_skip = jnp.repeat(ssd_d[0], HEAD_DIM).reshape(1, d_inner)
    lat = _ssd_out_proj(y, xbc, z, d_skip, ssd_norm_w[0].reshape(1, d_inner), tokens, mods, 1,
                        ssd_w_out[0].astype(BF16), n_tok=t_lat, seq=seq)

    ys, pos_blocks, wts = _moe(lat, t_lat, mods, 1, norm_ffn_w[1:2], rg_w[1], rg_b[1], re_w[1], re_b[1],
                               moe_w_gate, moe_w_up, moe_w_down, seq=seq)
    out = _final_norm(lat, ys, pos_blocks, wts, mods, 1, final_norm_w.reshape(1, d), seq=seq)
    return out.reshape(n_batch, seq, d)
```

```python
import functools

import jax
import jax.numpy as jnp
from jax import lax
from jax.experimental import pallas as pl
from jax.experimental.pallas import tpu as pltpu

F32 = jnp.float32
BF16 = jnp.bfloat16
I32 = jnp.int32
U32 = jnp.uint32

EPS = 1e-6
GRID_W = 64
HEAD_DIM = 64
D_STATE = 128
SSD_GROUPS = 8
CHUNK = 128
TOP_K = 2

LANES = 128
SUBLANES = 8
VMEM_LIMIT = 58 * 1024 * 1024
TOKEN_TILE = 512
COL_TILE = 512
OUT_TOKEN_TILE = 256
GROUP_TILE = 256
PERM_TILE = 512
ADA_COL_TILE = 1024
META_ROWS = 8
HI_MASK = 0xFFFF0000
LOG2_E = 1.4426950408889634


def _params(*sem):
    return pltpu.CompilerParams(dimension_semantics=sem, vmem_limit_bytes=VMEM_LIMIT)


def _dot(a, b):
    return jnp.dot(a, b, preferred_element_type=F32)


def _silu(x):
    return x * jax.nn.sigmoid(x)


def _mod_norm(x, nw, shift, scale):
    ms = jnp.mean(x * x, axis=-1, keepdims=True)
    y = x * lax.rsqrt(ms + EPS)
    return (y * nw) * (1.0 + scale) + shift


def _pack_pair(lo, hi):
    lo_bits = lax.bitcast_convert_type(lo.astype(BF16).astype(F32), U32)
    hi_bits = lax.bitcast_convert_type(hi.astype(BF16).astype(F32), U32)
    return (lo_bits >> 16) | (hi_bits & jnp.uint32(HI_MASK))


def _unpack_pair(u):
    lo = lax.bitcast_convert_type(u << 16, F32)
    hi = lax.bitcast_convert_type(u & jnp.uint32(HI_MASK), F32)
    return lo, hi


def _pack_row(x):
    half = x.shape[1] // 2
    return _pack_pair(x[:, :half], x[:, half:])


def _unpack_row(u):
    lo, hi = _unpack_pair(u)
    return jnp.concatenate([lo, hi], axis=1)


def _store_token_tiles(ref, packed):
    rows, width = packed.shape
    s = width // LANES
    for c in range(s):
        ref[pl.ds(c, rows, stride=s), :] = packed[:, c * LANES:(c + 1) * LANES]


def _load_token_tiles(ref, s):
    rows = ref.shape[0] // s
    return jnp.concatenate([ref[pl.ds(c, rows, stride=s), :] for c in range(s)], axis=1)


def _conv_masks(mask_ref, row_len):
    n_side, tm, _ = mask_ref.shape
    half = n_side // 2
    pos = lax.broadcasted_iota(I32, (tm, LANES), 0) & (row_len - 1)
    for m in range(n_side):
        off = m - half if m < half else m - half + 1
        valid = (pos + off >= 0) & (pos + off < row_len)
        mask_ref[m] = jnp.where(valid, 1.0, 0.0)


def _dwconv_rows(u, cw_ref, mask_ref, col0):
    tm, width_c = u.shape
    width = cw_ref.shape[0]
    half = width // 2
    outs = []
    for c in range(width_c // LANES):
        sl = slice(c * LANES, (c + 1) * LANES)
        wsl = slice(col0 + c * LANES, col0 + (c + 1) * LANES)
        uc = u[:, sl]
        out = cw_ref[half:half + 1, wsl] * uc
        for k in range(width):
            off = k - half
            if off == 0:
                continue
            m = k if k < half else k - 1
            shifted = pltpu.roll(uc, (-off) % tm, axis=0)
            out = out + cw_ref[k:k + 1, wsl] * (shifted * mask_ref[m])
        outs.append(out)
    return outs[0] if len(outs) == 1 else jnp.concatenate(outs, axis=1)


def _ada_kernel(c_ref, w_ref, b_ref, o_ref):
    s = _silu(c_ref[...])
    o_ref[...] = jnp.dot(s, w_ref[...], preferred_element_type=F32,
                         precision=lax.Precision.HIGHEST) + b_ref[...]


def _ada_mods(cvec, ada_w, ada_b):
    depth, d, d6 = ada_w.shape
    tn = min(ADA_COL_TILE, d6)
    out = pl.pallas_call(
        _ada_kernel,
        grid=(depth, d6 // tn),
        in_specs=[pl.BlockSpec((SUBLANES, d), lambda l, j: (0, 0)),
                  pl.BlockSpec((None, d, tn), lambda l, j: (l, 0, j)),
                  pl.BlockSpec((None, 1, tn), lambda l, j: (l, 0, j))],
        out_specs=pl.BlockSpec((None, SUBLANES, tn), lambda l, j: (l, 0, j)),
        out_shape=jax.ShapeDtypeStruct((depth, SUBLANES, d6), F32),
        compiler_params=_params("arbitrary", "arbitrary"),
        name="ada_mods",
    )(cvec, ada_w, ada_b.reshape(depth, 1, d6))
    return out.reshape(depth, SUBLANES, 6, d)


def _mod_row_fn(mods, tokens_per_batch, tile):
    n_rows = mods.shape[1]
    per_batch = tokens_per_batch // tile

    def mod_row(i):
        return jnp.minimum(i // per_batch, n_rows - 1)
    return mod_row


def _sc_kernel(xl_ref, xc_ref, m_ref, nw_ref, wb_ref, wc_ref, wv_ref, cw_ref, wo_ref,
               o_ref, h_ref, acc_ref, mask_ref, *, n_lat_tiles, ctx_row):
    i = pl.program_id(0)
    j = pl.program_id(1)
    is_lat = i < n_lat_tiles

    def for_source(fn):
        pl.when(is_lat)(lambda: fn(xl_ref))
        pl.when(jnp.logical_not(is_lat))(lambda: fn(xc_ref))

    @pl.when(j == 0)
    def _():
        def prologue(x_ref):
            h_ref[...] = _mod_norm(x_ref[...], nw_ref[...], m_ref[0:1, :], m_ref[1:2, :]).astype(BF16)
        for_source(prologue)
        acc_ref[...] = jnp.zeros_like(acc_ref)
        _conv_masks(mask_ref, jnp.where(is_lat, GRID_W, ctx_row))

    h = h_ref[...]
    gate_b = _dot(h, wb_ref[...])
    u = _dot(h, wc_ref[...]) * _dot(h, wv_ref[...])
    g = (gate_b * _dwconv_rows(u, cw_ref, mask_ref, 0)).astype(BF16)
    acc_ref[...] += _dot(g, wo_ref[...])

    @pl.when(j == pl.num_programs(1) - 1)
    def _():
        def epilogue(x_ref):
            o_ref[...] = x_ref[...] + m_ref[2:3, :] * acc_ref[...]
        for_source(epilogue)


def _short_conv_layer(x_lat, x_ctx, mods, layer, nw, w_in, conv_w, w_out, *, seq, ctx_row):
    t_lat, d = x_lat.shape
    t_ctx = x_ctx.shape[0]
    tm, tc = TOKEN_TILE, min(COL_TILE, d)
    n_lat, n_ctx, nj = t_lat // tm, t_ctx // tm, d // tc
    mod_row = _mod_row_fn(mods, seq, tm)
    kern = functools.partial(_sc_kernel, n_lat_tiles=n_lat, ctx_row=ctx_row)
    return pl.pallas_call(
        kern,
        grid=(n_lat + n_ctx, nj),
        in_specs=[
            pl.BlockSpec((tm, d), lambda i, j: (jnp.minimum(i, n_lat - 1), 0)),
            pl.BlockSpec((tm, d), lambda i, j: (jnp.maximum(i - n_lat, 0), 0)),
            pl.BlockSpec((None, None, 6, d), lambda i, j: (layer, mod_row(i), 0, 0)),
            pl.BlockSpec((1, d), lambda i, j: (0, 0)),
            pl.BlockSpec((d, tc), lambda i, j: (0, j)),
            pl.BlockSpec((d, tc), lambda i, j: (0, nj + j)),
            pl.BlockSpec((d, tc), lambda i, j: (0, 2 * nj + j)),
            pl.BlockSpec((conv_w.shape[0], tc), lambda i, j: (0, j)),
            pl.BlockSpec((tc, d), lambda i, j: (j, 0)),
        ],
        out_specs=pl.BlockSpec((tm, d), lambda i, j: (i, 0)),
        out_shape=jax.ShapeDtypeStruct((t_lat + t_ctx, d), F32),
        scratch_shapes=[pltpu.VMEM((tm, d), BF16), pltpu.VMEM((tm, d), F32),
                        pltpu.VMEM((conv_w.shape[0] - 1, tm, LANES), F32)],
        compiler_params=_params("arbitrary", "arbitrary"),
        name="short_conv_mixer",
    )(x_lat, x_ctx, mods, nw, w_in, w_in, w_in, conv_w, w_out)


def _route_kernel(x_ref, m_ref, nw_ref, wr_ref, br_ref, fp_ref, meta_ref, wts_ref, cnt_ref, run_ref,
                  *, n_groups, per_group):
    i = pl.program_id(0)

    @pl.when(i == 0)
    def _():
        run_ref[...] = jnp.zeros_like(run_ref)

    f = _mod_norm(x_ref[...], nw_ref[...], m_ref[3:4, :], m_ref[4:5, :])
    _store_token_tiles(fp_ref, _pack_row(f))
    logits = _dot(f.astype(BF16), wr_ref[...]) + br_ref[...]
    tm = logits.shape[0]
    lane = lax.broadcasted_iota(I32, logits.shape, 1).astype(F32)
    neg = -jnp.inf

    g_mask = lane < n_groups
    gl = jnp.where(g_mask, logits, neg)
    g_max = jnp.max(gl, axis=1, keepdims=True)
    g_sel = jnp.min(jnp.where(gl == g_max, lane, LANES), axis=1, keepdims=True)
    p_group = 1.0 / jnp.sum(jnp.where(g_mask, jnp.exp(gl - g_max), 0.0), axis=1, keepdims=True)

    e_lo = n_groups + g_sel * per_group
    in_group = (lane >= e_lo) & (lane < e_lo + per_group)
    el = jnp.where(in_group, logits, neg)
    e_max = jnp.max(el, axis=1, keepdims=True)
    ex = jnp.where(in_group, jnp.exp(el - e_max), 0.0)
    p = jnp.where(in_group, ex / jnp.sum(ex, axis=1, keepdims=True), -1.0)
    p1 = jnp.max(p, axis=1, keepdims=True)
    l1 = jnp.min(jnp.where(p == p1, lane, LANES), axis=1, keepdims=True)
    p_rest = jnp.where(lane == l1, -1.0, p)
    p2 = jnp.max(p_rest, axis=1, keepdims=True)
    l2 = jnp.min(jnp.where(p_rest == p2, lane, LANES), axis=1, keepdims=True)
    denom = p1 + p2
    w1 = p1 / denom * p_group
    w2 = p2 / denom * p_group

    hit1 = lane == l1
    hit2 = lane == l2
    onehot = jnp.where(hit1 | hit2, 1.0, 0.0)
    earlier = lax.broadcasted_iota(I32, (tm, tm), 0) > lax.broadcasted_iota(I32, (tm, tm), 1)
    before = _dot(jnp.where(earlier, 1.0, 0.0).astype(BF16), onehot.astype(BF16)) + run_ref[...]
    r1 = jnp.sum(jnp.where(hit1, before, 0.0), axis=1, keepdims=True)
    r2 = jnp.sum(jnp.where(hit2, before, 0.0), axis=1, keepdims=True)
    run_ref[...] += jnp.sum(onehot, axis=0, keepdims=True)
    cnt_ref[...] = run_ref[...].astype(I32)

    meta = jnp.where(lane == 0, l1 - n_groups,
                     jnp.where(lane == 1, l2 - n_groups, jnp.where(lane == 2, r1, r2)))
    meta_ref[...] = meta.T[0:META_ROWS, :].astype(I32)
    wts_ref[...] = jnp.where(lane == 0, w1, w2)


def _moe_route(tokens, n_tok, mods, layer, nw, wr, br, *, seq, n_groups, per_group):
    d = tokens.shape[1]
    tm = TOKEN_TILE
    s = d // 2 // LANES
    mod_row = _mod_row_fn(mods, seq, tm)
    kern = functools.partial(_route_kernel, n_groups=n_groups, per_group=per_group)
    return pl.pallas_call(
        kern,
        grid=(n_tok // tm,),
        in_specs=[
            pl.BlockSpec((tm, d), lambda i: (i, 0)),
            pl.BlockSpec((None, None, 6, d), lambda i: (layer, mod_row(i), 0, 0)),
            pl.BlockSpec((1, d), lambda i: (0, 0)),
            pl.BlockSpec((d, LANES), lambda i: (0, 0)),
            pl.BlockSpec((1, LANES), lambda i: (0, 0)),
        ],
        out_specs=[
            pl.BlockSpec((tm * s, LANES), lambda i: (i, 0)),
            pl.BlockSpec((META_ROWS, tm), lambda i: (i, 0)),
            pl.BlockSpec((tm, LANES), lambda i: (i, 0)),
            pl.BlockSpec((1, LANES), lambda i: (0, 0)),
        ],
        out_shape=[
            jax.ShapeDtypeStruct((n_tok * s, LANES), U32),
            jax.ShapeDtypeStruct((n_tok // tm * META_ROWS, tm), I32),
            jax.ShapeDtypeStruct((n_tok, LANES), F32),
            jax.ShapeDtypeStruct((1, LANES), I32),
        ],
        scratch_shapes=[pltpu.VMEM((1, LANES), F32)],
        compiler_params=_params("arbitrary"),
        name="moe_route",
    )(tokens, mods, nw, wr, br)


def _route_plan(counts, *, n_tok, n_exp, n_groups, tg):
    cnt = counts[0, n_groups:n_groups + n_exp]
    ends = jnp.cumsum(cnt)
    offs = ends - cnt
    n_row_tiles = TOP_K * n_tok // tg
    first_tile = offs // tg
    n_it = jnp.where(cnt > 0, (ends - 1) // tg - first_tile + 1, 0)
    it_end = jnp.cumsum(n_it)
    it_start = it_end - n_it
    n_items = it_end[-1]
    w = jnp.minimum(jnp.arange(n_row_tiles + n_exp - 1, dtype=I32), n_items - 1)
    item_exp = jnp.sum(it_end[None, :] <= w[:, None], axis=1).astype(I32)
    item_tile = (first_tile[item_exp] + w - it_start[item_exp]).astype(I32)
    row_lo = jnp.maximum(offs[item_exp] - item_tile * tg, 0).astype(I32)
    row_hi = jnp.minimum(ends[item_exp] - item_tile * tg, tg).astype(I32)
    first = jnp.concatenate([jnp.ones((1,), I32), (item_tile[1:] != item_tile[:-1]).astype(I32)])
    return offs.astype(I32), (item_exp, item_tile, row_lo, row_hi, first, n_items.astype(I32).reshape(1))


PERM_SLOTS = 3


def _token_copy(src, src_tok, dst, dst_tok, s, sem):
    return pltpu.make_async_copy(src.at[pl.ds(pl.multiple_of(src_tok * s, s), s), :],
                                 dst.at[pl.ds(pl.multiple_of(dst_tok * s, s), s), :], sem)


def _sorted_row(offs_ref, meta_ref, k, r):
    return offs_ref[meta_ref[0, k, r]] + meta_ref[0, TOP_K + k, r]


def _sort_kernel(offs_ref, pos_ref, src_hbm, dst_hbm, buf, lsem, rsem, *, s):
    i = pl.program_id(0)
    n = pl.num_programs(0)
    tp = pos_ref.shape[2]
    rows = tp * s
    slot = i % PERM_SLOTS

    def tile_load(step, sl):
        return pltpu.make_async_copy(src_hbm.at[pl.ds(pl.multiple_of(step * rows, rows), rows), :], buf.at[sl],
                                     lsem.at[sl])

    def wait_rows(sl):
        for _ in range(TOP_K):
            pltpu.make_async_copy(buf.at[sl], dst_hbm.at[pl.ds(0, rows), :], rsem.at[sl]).wait()

    @pl.when(i == 0)
    def _():
        tile_load(0, 0).start()

    @pl.when(i + 1 < n)
    def _():
        tile_load(i + 1, (i + 1) % PERM_SLOTS).start()

    tile_load(i, slot).wait()

    def body(r, c):
        for k in range(TOP_K):
            _token_copy(buf.at[slot], r, dst_hbm, _sorted_row(offs_ref, pos_ref, k, r), s, rsem.at[slot]).start()
        return c
    lax.fori_loop(0, tp, body, 0, unroll=8)

    @pl.when(i >= 1)
    def _():
        wait_rows((i - 1) % PERM_SLOTS)

    @pl.when(i == n - 1)
    def _():
        wait_rows(slot)


def _expert_rows_for_tile(i, n, offs_ref, pos_ref, posn_ref, ys_hbm, ybuf, sem, s):
    slot = i % 2

    def request(p_ref, sl):
        tp = p_ref.shape[2]

        def body(r, c):
            for k in range(TOP_K):
                _token_copy(ys_hbm, _sorted_row(offs_ref, p_ref, k, r), ybuf.at[sl, k], r, s, sem.at[sl]).start()
            return c
        lax.fori_loop(0, tp, body, 0, unroll=8)

    @pl.when(i == 0)
    def _():
        request(pos_ref, 0)

    @pl.when(i + 1 < n)
    def _():
        request(posn_ref, 1 - slot)

    for k in range(TOP_K):
        pltpu.make_async_copy(ys_hbm.at[pl.ds(0, ybuf.shape[2]), :], ybuf.at[slot, k], sem.at[slot]).wait()
    return slot


def _unsort_kernel(offs_ref, pos_ref, src_hbm, dst_hbm, buf, rsem, wsem, *, s, n_blocks):
    i = pl.program_id(0)
    tp = pos_ref.shape[2]
    rows = tp * s
    slot = i % PERM_SLOTS

    def write_back(step, sl, k):
        return pltpu.make_async_copy(buf.at[sl, k], dst_hbm.at[k, pl.ds(pl.multiple_of(step * rows, rows), rows), :],
                                     wsem.at[sl])

    def wait_rows(sl):
        for k in range(TOP_K):
            pltpu.make_async_copy(src_hbm.at[pl.ds(0, rows), :], buf.at[sl, k], rsem.at[sl]).wait()

    def finish(step, sl):
        wait_rows(sl)
        for k in range(TOP_K):
            write_back(step, sl, k).start()

    def wait_write_back(step, sl):
        for k in range(TOP_K):
            write_back(step, sl, k).wait()

    @pl.when(i >= PERM_SLOTS)
    def _():
        wait_write_back(i - PERM_SLOTS, slot)

    def body(r, c):
        for k in range(TOP_K):
            _token_copy(src_hbm, _sorted_row(offs_ref, pos_ref, k, r), buf.at[slot, k], r, s, rsem.at[slot]).start()
        return c
    lax.fori_loop(0, tp, body, 0, unroll=8)

    @pl.when(i >= 1)
    def _():
        finish(i - 1, (i - 1) % PERM_SLOTS)

    @pl.when(i == n_blocks - 1)
    def _():
        finish(i, slot)
        for back in range(min(PERM_SLOTS, n_blocks)):
            wait_write_back(i - back, (i - back) % PERM_SLOTS)


def _unsort_rows(offs, meta_blocks, src, *, n_tok, s):
    n_blocks, _, tp = meta_blocks.shape
    return pl.pallas_call(
        functools.partial(_unsort_kernel, s=s, n_blocks=n_blocks),
        grid=(n_blocks,),
        in_specs=[pl.BlockSpec(memory_space=pltpu.SMEM),
                  pl.BlockSpec((1, META_ROWS, tp), lambda i: (i, 0, 0), memory_space=pltpu.SMEM),
                  pl.BlockSpec(memory_space=pl.ANY)],
        out_specs=pl.BlockSpec(memory_space=pl.ANY),
        out_shape=jax.ShapeDtypeStruct((TOP_K, n_tok * s, LANES), src.dtype),
        scratch_shapes=[pltpu.VMEM((PERM_SLOTS, TOP_K, tp * s, LANES), src.dtype),
                        pltpu.SemaphoreType.DMA((PERM_SLOTS,)), pltpu.SemaphoreType.DMA((PERM_SLOTS,))],
        compiler_params=_params("arbitrary"),
        name="moe_unsort_rows",
    )(offs, meta_blocks, src)


def _pos_specs(n_blocks, tp):
    return [pl.BlockSpec(memory_space=pltpu.SMEM),
            pl.BlockSpec((1, META_ROWS, tp), lambda i: (i, 0, 0), memory_space=pltpu.SMEM),
            pl.BlockSpec((1, META_ROWS, tp), lambda i: (jnp.minimum(i + 1, n_blocks - 1), 0, 0),
                         memory_space=pltpu.SMEM)]


def _sort_rows(offs, meta_blocks, src, *, n_tok, s):
    n_blocks, _, tp = meta_blocks.shape
    return pl.pallas_call(
        functools.partial(_sort_kernel, s=s),
        grid=(n_blocks,),
        in_specs=[pl.BlockSpec(memory_space=pltpu.SMEM),
                  pl.BlockSpec((1, META_ROWS, tp), lambda i: (i, 0, 0), memory_space=pltpu.SMEM),
                  pl.BlockSpec(memory_space=pl.ANY)],
        out_specs=pl.BlockSpec(memory_space=pl.ANY),
        out_shape=jax.ShapeDtypeStruct((TOP_K * n_tok * s, LANES), src.dtype),
        scratch_shapes=[pltpu.VMEM((PERM_SLOTS, tp * s, LANES), src.dtype),
                        pltpu.SemaphoreType.DMA((PERM_SLOTS,)), pltpu.SemaphoreType.DMA((PERM_SLOTS,))],
        compiler_params=_params("arbitrary"),
        name="moe_sort_rows",
    )(offs, meta_blocks, src)


def _gmm_kernel(ie_ref, it_ref, lo_ref, hi_ref, first_ref, n_ref, xs_ref, wg_ref, wu_ref, wd_ref, ys_ref,
                wg_b, wu_b, wd_b):
    w = pl.program_id(0)

    @pl.when(w < n_ref[0])
    def _():
        @pl.when((w == 0) | (ie_ref[w] != ie_ref[jnp.maximum(w - 1, 0)]))
        def _():
            wg_b[...] = wg_ref[...].astype(BF16)
            wu_b[...] = wu_ref[...].astype(BF16)
            wd_b[...] = wd_ref[...].astype(BF16)

        half = wg_b.shape[0] // 2
        s = half // LANES
        x_lo, x_hi = _unpack_pair(_load_token_tiles(xs_ref, s))
        x_lo, x_hi = x_lo.astype(BF16), x_hi.astype(BF16)
        hg = _dot(x_lo, wg_b[0:half, :]) + _dot(x_hi, wg_b[half:, :])
        hu = _dot(x_lo, wu_b[0:half, :]) + _dot(x_hi, wu_b[half:, :])
        yp = _pack_row(_dot((_silu(hg) * hu).astype(BF16), wd_b[...]))
        rows = lax.broadcasted_iota(I32, yp.shape, 0)
        mine = (rows >= lo_ref[w]) & (rows < hi_ref[w])

        @pl.when(first_ref[w] == 1)
        def _():
            _store_token_tiles(ys_ref, jnp.where(mine, yp, jnp.uint32(0)))

        @pl.when(first_ref[w] == 0)
        def _():
            _store_token_tiles(ys_ref, jnp.where(mine, yp, _load_token_tiles(ys_ref, s)))


def _moe_experts(xs, plan, layer, wg, wu, wd):
    _, _, d, de = wg.shape
    s = d // 2 // LANES
    tg = GROUP_TILE
    n_items_max = plan[0].shape[0]
    grid_spec = pltpu.PrefetchScalarGridSpec(
        num_scalar_prefetch=6,
        grid=(n_items_max,),
        in_specs=[
            pl.BlockSpec((tg * s, LANES), lambda w, ie, it, lo, hi, fi, n: (it[w], 0)),
            pl.BlockSpec((None, None, d, de), lambda w, ie, it, lo, hi, fi, n: (layer, ie[w], 0, 0)),
            pl.BlockSpec((None, None, d, de), lambda w, ie, it, lo, hi, fi, n: (layer, ie[w], 0, 0)),
            pl.BlockSpec((None, None, de, d), lambda w, ie, it, lo, hi, fi, n: (layer, ie[w], 0, 0)),
        ],
        out_specs=pl.BlockSpec((tg * s, LANES), lambda w, ie, it, lo, hi, fi, n: (it[w], 0)),
        scratch_shapes=[pltpu.VMEM((d, de), BF16), pltpu.VMEM((d, de), BF16), pltpu.VMEM((de, d), BF16)],
    )
    return pl.pallas_call(
        _gmm_kernel,
        grid_spec=grid_spec,
        out_shape=jax.ShapeDtypeStruct(xs.shape, U32),
        compiler_params=_params("arbitrary"),
        name="moe_experts",
    )(*plan, xs, wg, wu, wd)


def _moe_combine(y0_ref, y1_ref, wts_ref, s):
    wts = wts_ref[...]
    return (wts[:, 0:1] * _unpack_row(_load_token_tiles(y0_ref, s))
            + wts[:, 1:2] * _unpack_row(_load_token_tiles(y1_ref, s)))


def _inproj_kernel(x_ref, y0_ref, y1_ref, wts_ref, mp_ref, m_ref, nw_ref, w_ref, wdt_ref, cw_ref, cb_ref, dtb_ref,
                   lat_ref, z_ref, xbc_ref, dt_ref, h_ref, mask_ref, *, nz, n_lat_tiles, ctx_row):
    i = pl.program_id(0)
    j = pl.program_id(1)
    n_sub = w_ref.shape[1] // COL_TILE

    @pl.when(j == 0)
    def _():
        s = x_ref.shape[1] // 2 // LANES
        lat = x_ref[...] + mp_ref[5:6, :] * _moe_combine(y0_ref, y1_ref, wts_ref, s)
        lat_ref[...] = lat
        h_ref[...] = _mod_norm(lat, nw_ref[...], m_ref[0:1, :], m_ref[1:2, :]).astype(BF16)
        _conv_masks(mask_ref, jnp.where(i < n_lat_tiles, GRID_W, ctx_row))

    @pl.when(j < nz)
    def _():
        for s in range(n_sub):
            sl = slice(s * COL_TILE, (s + 1) * COL_TILE)
            z_ref[:, sl] = _dot(h_ref[...], w_ref[:, sl]).astype(BF16)

    @pl.when(j >= nz)
    def _():
        for s in range(n_sub):
            sl = slice(s * COL_TILE, (s + 1) * COL_TILE)
            r = _dot(h_ref[...], w_ref[:, sl])
            xbc_ref[:, sl] = _silu(_dwconv_rows(r, cw_ref, mask_ref, s * COL_TILE) + cb_ref[:, sl]).astype(BF16)

    @pl.when(j == pl.num_programs(1) - 1)
    def _():
        v = _dot(h_ref[...], wdt_ref[...]) + dtb_ref[...]
        dt_ref[...] = jnp.maximum(v, 0.0) + jnp.log1p(jnp.exp(-jnp.abs(v)))


def _ssd_in_proj(tokens, y2, wts, mods, layer, nw, w_zx, w_dt, conv_w, conv_b, dt_bias,
                 *, seq, n_lat_tiles, ctx_row, d_inner, conv_dim):
    t, d = tokens.shape
    tm = TOKEN_TILE
    tc = next(c for c in (4 * COL_TILE, 2 * COL_TILE, COL_TILE) if d_inner % c == 0 and conv_dim % c == 0)
    nz, nx = d_inner // tc, conv_dim // tc
    mod_row = _mod_row_fn(mods, seq, tm)

    def xcol(j):
        return jnp.maximum(j - nz, 0)

    kern = functools.partial(_inproj_kernel, nz=nz, n_lat_tiles=n_lat_tiles, ctx_row=ctx_row)
    return pl.pallas_call(
        kern,
        grid=(t // tm, nz + nx),
        in_specs=[
            pl.BlockSpec((tm, d), lambda i, j: (i, 0)),
            pl.BlockSpec((None, tm * (d // 2 // LANES), LANES), lambda i, j: (0, i, 0)),
            pl.BlockSpec((None, tm * (d // 2 // LANES), LANES), lambda i, j: (1, i, 0)),
            pl.BlockSpec((tm, LANES), lambda i, j: (i, 0)),
            pl.BlockSpec((None, None, 6, d), lambda i, j: (layer - 1, mod_row(i), 0, 0)),
            pl.BlockSpec((None, None, 6, d), lambda i, j: (layer, mod_row(i), 0, 0)),
            pl.BlockSpec((1, d), lambda i, j: (0, 0)),
            pl.BlockSpec((d, tc), lambda i, j: (0, j)),
            pl.BlockSpec((d, LANES), lambda i, j: (0, 0)),
            pl.BlockSpec((conv_w.shape[0], tc), lambda i, j: (0, xcol(j))),
            pl.BlockSpec((1, tc), lambda i, j: (0, xcol(j))),
            pl.BlockSpec((1, LANES), lambda i, j: (0, 0)),
        ],
        out_specs=[
            pl.BlockSpec((tm, d), lambda i, j: (i, 0)),
            pl.BlockSpec((tm, tc), lambda i, j: (i, jnp.minimum(j, nz - 1))),
            pl.BlockSpec((tm, tc), lambda i, j: (i, xcol(j))),
            pl.BlockSpec((tm, LANES), lambda i, j: (i, 0)),
        ],
        out_shape=[
            jax.ShapeDtypeStruct((t, d), F32),
            jax.ShapeDtypeStruct((t, d_inner), BF16),
            jax.ShapeDtypeStruct((t, conv_dim), BF16),
            jax.ShapeDtypeStruct((t, LANES), F32),
        ],
        scratch_shapes=[pltpu.VMEM((tm, d), BF16), pltpu.VMEM((conv_w.shape[0] - 1, tm, LANES), F32)],
        compiler_params=_params("arbitrary", "arbitrary"),
        name="ssd_in_proj",
    )(tokens, y2, y2, wts, mods, mods, nw, w_zx, w_dt, conv_w, conv_b, dt_bias)


def _ssd_kernel(x_ref, b_ref, c_ref, dt_ref, a_ref, y_ref, st_ref, *, n_heads):
    d = pl.program_id(1)
    k = pl.program_id(2)
    q = x_ref.shape[0]
    n_groups = st_ref.shape[0]
    gw = st_ref.shape[2]
    hpg = gw // HEAD_DIM
    pw = 2 * HEAD_DIM
    fwd = d == 0

    @pl.when(k == 0)
    def _():
        st_ref[...] = jnp.zeros_like(st_ref)

    t_i = lax.broadcasted_iota(I32, (q, q), 0)
    s_i = lax.broadcasted_iota(I32, (q, q), 1)
    incl = jnp.where(fwd, t_i - s_i, s_i - t_i) >= 0

    dt_all = dt_ref[...]
    dta_all = dt_all * (a_ref[...] * LOG2_E)
    cum_all = jnp.dot(jnp.where(incl, 1.0, 0.0), dta_all, preferred_element_type=F32,
                      precision=lax.Precision.HIGHEST)
    cum_t_all = cum_all.T
    dt_t_all = dt_all.T
    tot_t_all = jnp.sum(dta_all.T, axis=1, keepdims=True)

    def pick_cols(a):
        return jnp.where(fwd, a[:, 0:n_heads], a[:, n_heads:2 * n_heads])

    def pick_rows(a):
        return jnp.where(fwd, a[0:n_heads, :], a[n_heads:2 * n_heads, :])

    cum = pick_cols(cum_all)
    tot = pick_cols(jnp.sum(dta_all, axis=0, keepdims=True))
    cum_t = pick_rows(cum_t_all)
    dt_t = pick_rows(dt_t_all)
    to_end_t = (dt_t * jnp.exp2(pick_rows(tot_t_all) - cum_t)).astype(BF16)
    dt_tb = dt_t.astype(BF16)
    dec_all = jnp.exp2(tot)

    low_half = lax.broadcasted_iota(I32, (q, pw), 1) < HEAD_DIM
    low_row = lax.broadcasted_iota(I32, (1, pw), 1) < HEAD_DIM

    for g in range(n_groups):
        bg = b_ref[:, g * D_STATE:(g + 1) * D_STATE]
        cg = c_ref[:, g * D_STATE:(g + 1) * D_STATE]
        cb = lax.dot_general(cg, bg, (((1,), (1,)), ((), ())), preferred_element_type=F32)
        cb = jnp.where(incl, cb, 0.0).astype(BF16)
        bg_t = bg.astype(F32).T.astype(BF16)
        st = st_ref[g]
        y_off = _dot(cg, st.astype(BF16))

        for pair in range(hpg // 2):
            psl = slice(pair * pw, (pair + 1) * pw)
            xp = x_ref[:, g * gw + pair * pw:g * gw + (pair + 1) * pw]
            zero = jnp.zeros_like(xp)
            rhs = jnp.concatenate([jnp.where(low_half, xp, zero), jnp.where(low_half, zero, xp)], axis=0)
            w_parts, u_parts, cols = [], [], []
            for hh in range(2):
                h = g * hpg + pair * 2 + hh
                col = jnp.broadcast_to(cum[:, h:h + 1], (q, q))
                seg = jnp.exp2(jnp.minimum(col - cum_t[h:h + 1, :], 0.0))
                w_parts.append(cb * seg.astype(BF16) * dt_tb[h:h + 1, :])
                u_parts.append(bg_t * to_end_t[h:h + 1, :])
                cols.append(col)
            y_diag = _dot(jnp.concatenate(w_parts, axis=1), rhs)
            upd = _dot(jnp.concatenate(u_parts, axis=1), rhs)
            dec_in = jnp.exp2(jnp.where(low_half, cols[0], cols[1]))
            y_ref[:, g * gw + pair * pw:g * gw + (pair + 1) * pw] = (y_diag + y_off[:, psl] * dec_in).astype(BF16)
            h0 = g * hpg + pair * 2
            dec_pair = jnp.where(low_row, jnp.broadcast_to(dec_all[:, h0:h0 + 1], (1, pw)),
                                 jnp.broadcast_to(dec_all[:, h0 + 1:h0 + 2], (1, pw)))
            st_ref[g, :, psl] = st[:, psl] * dec_pair + upd


def _ssd_scan(xbc, dt, a_neg, *, n_batch, lat_chunks, ctx_chunks, d_inner, n_heads):
    t = xbc.shape[0]
    q = CHUNK
    gn = SSD_GROUPS * D_STATE
    gw = d_inner // SSD_GROUPS
    assert gw % (2 * HEAD_DIM) == 0 and d_inner % gn == 0 and q == 2 * HEAD_DIM and q == LANES
    n_steps = ctx_chunks + lat_chunks
    ctx_base = n_batch * lat_chunks

    def chunk(b, d, k):
        kc = jnp.where(d == 0, k, ctx_chunks - 1 - k)
        kl = jnp.where(d == 0, k - ctx_chunks, n_steps - 1 - k)
        return jnp.where(k < ctx_chunks, ctx_base + b * ctx_chunks + kc, b * lat_chunks + kl)

    kern = functools.partial(_ssd_kernel, n_heads=n_heads)
    return pl.pallas_call(
        kern,
        grid=(n_batch, 2, n_steps),
        in_specs=[
            pl.BlockSpec((q, d_inner), lambda b, d, k: (chunk(b, d, k), 0)),
            pl.BlockSpec((q, gn), lambda b, d, k: (chunk(b, d, k), d_inner // gn)),
            pl.BlockSpec((q, gn), lambda b, d, k: (chunk(b, d, k), d_inner // gn + 1)),
            pl.BlockSpec((q, LANES), lambda b, d, k: (chunk(b, d, k), 0)),
            pl.BlockSpec((1, LANES), lambda b, d, k: (0, 0)),
        ],
        out_specs=pl.BlockSpec((None, q, d_inner), lambda b, d, k: (d, chunk(b, d, k), 0)),
        out_shape=jax.ShapeDtypeStruct((2, t, d_inner), BF16),
        scratch_shapes=[pltpu.VMEM((SSD_GROUPS, D_STATE, gw), F32)],
        compiler_params=_params("arbitrary", "arbitrary", "arbitrary"),
        name="ssd_scan",
    )(xbc, xbc, xbc, dt, a_neg)


def _outproj_kernel(yf_ref, yb_ref, xs_ref, z_ref, dsk_ref, gnw_ref, lat_ref, m_ref, w_ref, o_ref):
    gw = w_ref.shape[0] // SSD_GROUPS
    acc = None
    for g in range(SSD_GROUPS):
        sl = slice(g * gw, (g + 1) * gw)
        y = (yf_ref[:, sl].astype(F32) + yb_ref[:, sl].astype(F32)
             + dsk_ref[:, sl] * xs_ref[:, sl].astype(F32))
        gated = y * _silu(z_ref[:, sl].astype(F32))
        ms = jnp.mean(gated * gated, axis=-1, keepdims=True)
        gn = (gated * lax.rsqrt(ms + EPS) * gnw_ref[:, sl]).astype(BF16)
        part = _dot(gn, w_ref[sl, :])
        acc = part if acc is None else acc + part
    o_ref[...] = lat_ref[...] + m_ref[2:3, :] * acc


def _ssd_out_proj(y, xbc, z, d_skip, gnw, tokens, mods, layer, w_out, *, n_tok, seq):
    d_inner, d = w_out.shape
    tm = OUT_TOKEN_TILE
    mod_row = _mod_row_fn(mods, seq, tm)
    return pl.pallas_call(
        _outproj_kernel,
        grid=(n_tok // tm,),
        in_specs=[
            pl.BlockSpec((None, tm, d_inner), lambda i: (0, i, 0)),
            pl.BlockSpec((None, tm, d_inner), lambda i: (1, i, 0)),
            pl.BlockSpec((tm, d_inner), lambda i: (i, 0)),
            pl.BlockSpec((tm, d_inner), lambda i: (i, 0)),
            pl.BlockSpec((1, d_inner), lambda i: (0, 0)),
            pl.BlockSpec((1, d_inner), lambda i: (0, 0)),
            pl.BlockSpec((tm, d), lambda i: (i, 0)),
            pl.BlockSpec((None, None, 6, d), lambda i: (layer, mod_row(i), 0, 0)),
            pl.BlockSpec((d_inner, d), lambda i: (0, 0), pipeline_mode=pl.Buffered(1)),
        ],
        out_specs=pl.BlockSpec((tm, d), lambda i: (i, 0)),
        out_shape=jax.ShapeDtypeStruct((n_tok, d), F32),
        compiler_params=_params("arbitrary"),
        name="ssd_out_proj",
    )(y, y, xbc, z, d_skip, gnw, tokens, mods, w_out)


def _final_kernel(offs_ref, pos_ref, posn_ref, x_ref, ys_hbm, wts_ref, m_ref, nw_ref, o_ref, ybuf, ysem):
    s = x_ref.shape[1] // 2 // LANES
    slot = _expert_rows_for_tile(pl.program_id(0), pl.num_programs(0), offs_ref, pos_ref, posn_ref, ys_hbm, ybuf,
                                 ysem, s)
    lat = x_ref[...] + m_ref[5:6, :] * _moe_combine(ybuf.at[slot, 0], ybuf.at[slot, 1], wts_ref, s)
    ms = jnp.mean(lat * lat, axis=-1, keepdims=True)
    o_ref[...] = lat * lax.rsqrt(ms + EPS) * nw_ref[...]


def _final_norm(tokens, ys, offs, meta_blocks, wts, mods, layer, nw, *, seq):
    t, d = tokens.shape
    tm = TOKEN_TILE
    s = d // 2 // LANES
    assert meta_blocks.shape == (t // tm, META_ROWS, tm)
    mod_row = _mod_row_fn(mods, seq, tm)
    return pl.pallas_call(
        _final_kernel,
        grid=(t // tm,),
        in_specs=_pos_specs(t // tm, tm) + [
            pl.BlockSpec((tm, d), lambda i: (i, 0)),
            pl.BlockSpec(memory_space=pl.ANY),
            pl.BlockSpec((tm, LANES), lambda i: (i, 0)),
            pl.BlockSpec((None, None, 6, d), lambda i: (layer, mod_row(i), 0, 0)),
            pl.BlockSpec((1, d), lambda i: (0, 0)),
        ],
        out_specs=pl.BlockSpec((tm, d), lambda i: (i, 0)),
        out_shape=jax.ShapeDtypeStruct((t, d), F32),
        scratch_shapes=[pltpu.VMEM((2, TOP_K, tm * s, LANES), U32), pltpu.SemaphoreType.DMA((2,))],
        compiler_params=_params("arbitrary"),
        name="final_norm",
    )(offs, meta_blocks, meta_blocks, tokens, ys, wts, mods, nw)


def _router_weights(rg_w, rg_b, re_w, re_b):
    d, n_groups = rg_w.shape
    n_exp = re_w.shape[1]
    pad = LANES - n_groups - n_exp
    wr = jnp.concatenate([rg_w, re_w, jnp.zeros((d, pad), F32)], axis=1).astype(BF16)
    br = jnp.concatenate([rg_b, re_b, jnp.zeros((pad,), F32)]).reshape(1, LANES)
    return wr, br


def _moe(tokens, n_tok, mods, layer, nw, rg_w, rg_b, re_w, re_b, w_gate, w_up, w_down, *, seq):
    n_groups = rg_w.shape[1]
    n_exp = re_w.shape[1]
    s = tokens.shape[1] // 2 // LANES
    assert n_tok % PERM_TILE == 0 and (TOP_K * n_tok) % GROUP_TILE == 0 and tokens.shape[1] % (2 * LANES) == 0
    assert PERM_TILE == TOKEN_TILE
    wr, br = _router_weights(rg_w, rg_b, re_w, re_b)
    fp, meta, wts, counts = _moe_route(tokens, n_tok, mods, layer, nw, wr, br, seq=seq, n_groups=n_groups,
                                       per_group=n_exp // n_groups)
    offs, plan = _route_plan(counts, n_tok=n_tok, n_exp=n_exp, n_groups=n_groups, tg=GROUP_TILE)
    meta_blocks = meta.reshape(n_tok // PERM_TILE, META_ROWS, PERM_TILE)
    xs = _sort_rows(offs, meta_blocks, fp, n_tok=n_tok, s=s)
    return _moe_experts(xs, plan, layer, w_gate, w_up, w_down), offs, meta_blocks, wts


def kernel(x, c, ctx, c_ctx, ada_w, ada_b, norm_mix_w, norm_ffn_w, sc_w_in, sc_conv_w, sc_w_out, ssd_w_in,
           ssd_conv_w, ssd_conv_b, ssd_dt_bias, ssd_a_log, ssd_d, ssd_norm_w, ssd_w_out, rg_w, rg_b, re_w, re_b,
           moe_w_gate, moe_w_up, moe_w_down, final_norm_w):
    n_batch, seq, d = x.shape
    ctx_len = ctx.shape[1]
    depth = ada_w.shape[0]
    d_inner = ssd_w_out.shape[1]
    conv_dim = ssd_conv_w.shape[2]
    n_heads = ssd_d.shape[1]
    tm = TOKEN_TILE
    t_lat, t_ctx = n_batch * seq, n_batch * ctx_len
    assert depth == 2 and seq % tm == 0 and t_ctx % tm == 0 and tm % ctx_len == 0 and tm % GRID_W == 0
    assert ctx_len & (ctx_len - 1) == 0 and GRID_W & (GRID_W - 1) == 0
    assert seq % CHUNK == 0 and ctx_len % CHUNK == 0 and d_inner == n_heads * HEAD_DIM
    assert n_batch + 1 <= SUBLANES and 2 * n_heads <= LANES and d_inner % COL_TILE == 0 and conv_dim % COL_TILE == 0
    n_lat_tiles = t_lat // tm

    cvec = jnp.zeros((SUBLANES, d), F32).at[0:n_batch].set(c).at[n_batch].set(c_ctx)
    mods = _ada_mods(cvec, ada_w, ada_b)

    x_lat = x.reshape(t_lat, d)
    x_ctx = ctx.reshape(t_ctx, d)

    tokens = _short_conv_layer(x_lat, x_ctx, mods, 0, norm_mix_w[0:1], sc_w_in[0].astype(BF16), sc_conv_w[0],
                               sc_w_out[0].astype(BF16), seq=seq, ctx_row=ctx_len)
    t_all = t_lat + t_ctx
    ys, offs, meta_blocks, wts = _moe(tokens, t_all, mods, 0, norm_ffn_w[0:1], rg_w[0], rg_b[0], re_w[0], re_b[0],
                                      moe_w_gate, moe_w_up, moe_w_down, seq=seq)

    w_in = ssd_w_in[0]
    dt_cols = 2 * n_heads
    w_zx = w_in[:, :d_inner + conv_dim].astype(BF16)
    w_dt = jnp.concatenate([w_in[:, d_inner + conv_dim:], jnp.zeros((d, LANES - dt_cols), F32)],
                           axis=1).astype(BF16)
    dt_bias = jnp.zeros((1, LANES), F32).at[0, :dt_cols].set(ssd_dt_bias[0].reshape(-1))
    a_neg = jnp.zeros((1, LANES), F32).at[0, :dt_cols].set(-jnp.exp(ssd_a_log[0].reshape(-1)))
    y2 = _unsort_rows(offs, meta_blocks, ys, n_tok=t_all, s=d // 2 // LANES)
    tokens, z, xbc, dt = _ssd_in_proj(tokens, y2, wts, mods, 1, norm_mix_w[1:2], w_zx, w_dt, ssd_conv_w[0],
                                      ssd_conv_b[0].reshape(1, conv_dim), dt_bias, seq=seq,
                                      n_lat_tiles=n_lat_tiles, ctx_row=ctx_len, d_inner=d_inner, conv_dim=conv_dim)
    y = _ssd_scan(xbc, dt, a_neg, n_batch=n_batch, lat_chunks=seq // CHUNK, ctx_chunks=ctx_len // CHUNK,
                  d_inner=d_inner, n_heads=n_heads)
    d_skip = jnp.repeat(ssd_d[0], HEAD_DIM).reshape(1, d_inner)
    lat = _ssd_out_proj(y, xbc, z, d_skip, ssd_norm_w[0].reshape(1, d_inner), tokens, mods, 1,
                        ssd_w_out[0].astype(BF16), n_tok=t_lat, seq=seq)

    ys, offs, meta_blocks, wts = _moe(lat, t_lat, mods, 1, norm_ffn_w[1:2], rg_w[1], rg_b[1], re_w[1], re_b[1],
                                      moe_w_gate, moe_w_up, moe_w_down, seq=seq)
    out = _final_norm(lat, ys, offs, meta_blocks, wts, mods, 1, final_norm_w.reshape(1, d), seq=seq)
    return out.reshape(n_batch, seq, d)
```

```python
import functools

import jax
import jax.numpy as jnp
from jax import lax
from jax.experimental import pallas as pl
from jax.experimental.pallas import tpu as pltpu

F32 = jnp.float32
BF16 = jnp.bfloat16
I32 = jnp.int32
U32 = jnp.uint32

EPS = 1e-6
GRID_W = 64
HEAD_DIM = 64
D_STATE = 128
SSD_GROUPS = 8
CHUNK = 128
TOP_K = 2

LANES = 128
SUBLANES = 8
VMEM_LIMIT = 58 * 1024 * 1024
TOKEN_TILE = 512
COL_TILE = 512
OUT_TOKEN_TILE = 256
GROUP_TILE = 256
PERM_TILE = 512
ADA_COL_TILE = 1024
META_ROWS = 8
HI_MASK = 0xFFFF0000
LOG2_E = 1.4426950408889634


def _params(*sem):
    return pltpu.CompilerParams(dimension_semantics=sem, vmem_limit_bytes=VMEM_LIMIT)


def _dot(a, b):
    return jnp.dot(a, b, preferred_element_type=F32)


def _silu(x):
    return x * jax.nn.sigmoid(x)


def _mod_norm(x, nw, shift, scale):
    ms = jnp.mean(x * x, axis=-1, keepdims=True)
    y = x * lax.rsqrt(ms + EPS)
    return (y * nw) * (1.0 + scale) + shift


def _pack_pair(lo, hi):
    lo_bits = lax.bitcast_convert_type(lo.astype(BF16).astype(F32), U32)
    hi_bits = lax.bitcast_convert_type(hi.astype(BF16).astype(F32), U32)
    return (lo_bits >> 16) | (hi_bits & jnp.uint32(HI_MASK))


def _unpack_pair(u):
    lo = lax.bitcast_convert_type(u << 16, F32)
    hi = lax.bitcast_convert_type(u & jnp.uint32(HI_MASK), F32)
    return lo, hi


def _pack_row(x):
    half = x.shape[1] // 2
    return _pack_pair(x[:, :half], x[:, half:])


def _unpack_row(u):
    lo, hi = _unpack_pair(u)
    return jnp.concatenate([lo, hi], axis=1)


def _store_token_tiles(ref, packed):
    rows, width = packed.shape
    s = width // LANES
    for c in range(s):
        ref[pl.ds(c, rows, stride=s), :] = packed[:, c * LANES:(c + 1) * LANES]


def _load_token_tiles(ref, s):
    rows = ref.shape[0] // s
    return jnp.concatenate([ref[pl.ds(c, rows, stride=s), :] for c in range(s)], axis=1)


def _conv_masks(mask_ref, row_len):
    n_side, tm, _ = mask_ref.shape
    half = n_side // 2
    pos = lax.broadcasted_iota(I32, (tm, LANES), 0) & (row_len - 1)
    for m in range(n_side):
        off = m - half if m < half else m - half + 1
        valid = (pos + off >= 0) & (pos + off < row_len)
        mask_ref[m] = jnp.where(valid, 1.0, 0.0)


def _dwconv_rows(u, cw_ref, mask_ref, col0):
    tm, width_c = u.shape
    width = cw_ref.shape[0]
    half = width // 2
    outs = []
    for c in range(width_c // LANES):
        sl = slice(c * LANES, (c + 1) * LANES)
        wsl = slice(col0 + c * LANES, col0 + (c + 1) * LANES)
        uc = u[:, sl]
        out = cw_ref[half:half + 1, wsl] * uc
        for k in range(width):
            off = k - half
            if off == 0:
                continue
            m = k if k < half else k - 1
            shifted = pltpu.roll(uc, (-off) % tm, axis=0)
            out = out + cw_ref[k:k + 1, wsl] * (shifted * mask_ref[m])
        outs.append(out)
    return outs[0] if len(outs) == 1 else jnp.concatenate(outs, axis=1)


def _dwconv_rows_static(u, cw_ref, col0, row_len):
    tm, width_c = u.shape
    width = cw_ref.shape[0]
    half = width // 2
    n_rows = tm // row_len
    outs = []
    for c in range(width_c // LANES):
        wsl = slice(col0 + c * LANES, col0 + (c + 1) * LANES)
        uc = u[:, c * LANES:(c + 1) * LANES]
        u3 = uc.reshape(n_rows, row_len, LANES)
        out = cw_ref[half:half + 1, wsl] * uc
        for k in range(width):
            off = k - half
            if off == 0:
                continue
            fill = jnp.zeros((n_rows, abs(off), LANES), F32)
            sh3 = (jnp.concatenate([u3[:, off:, :], fill], axis=1) if off > 0
                   else jnp.concatenate([fill, u3[:, :off, :]], axis=1))
            out = out + cw_ref[k:k + 1, wsl] * sh3.reshape(tm, LANES)
        outs.append(out)
    return outs[0] if len(outs) == 1 else jnp.concatenate(outs, axis=1)


def _ada_kernel(c_ref, w_ref, b_ref, o_ref):
    s = _silu(c_ref[...])
    o_ref[...] = jnp.dot(s, w_ref[...], preferred_element_type=F32,
                         precision=lax.Precision.HIGHEST) + b_ref[...]


def _ada_mods(cvec, ada_w, ada_b):
    depth, d, d6 = ada_w.shape
    tn = min(ADA_COL_TILE, d6)
    out = pl.pallas_call(
        _ada_kernel,
        grid=(depth, d6 // tn),
        in_specs=[pl.BlockSpec((SUBLANES, d), lambda l, j: (0, 0)),
                  pl.BlockSpec((None, d, tn), lambda l, j: (l, 0, j)),
                  pl.BlockSpec((None, 1, tn), lambda l, j: (l, 0, j))],
        out_specs=pl.BlockSpec((None, SUBLANES, tn), lambda l, j: (l, 0, j)),
        out_shape=jax.ShapeDtypeStruct((depth, SUBLANES, d6), F32),
        compiler_params=_params("arbitrary", "arbitrary"),
        name="ada_mods",
    )(cvec, ada_w, ada_b.reshape(depth, 1, d6))
    return out.reshape(depth, SUBLANES, 6, d)


def _mod_row_fn(mods, tokens_per_batch, tile):
    n_rows = mods.shape[1]
    per_batch = tokens_per_batch // tile

    def mod_row(i):
        return jnp.minimum(i // per_batch, n_rows - 1)
    return mod_row


def _sc_kernel(xl_ref, xc_ref, m_ref, nw_ref, wb_ref, wc_ref, wv_ref, cw_ref, wo_ref,
               o_ref, h_ref, acc_ref, mask_ref, *, n_lat_tiles, ctx_row):
    i = pl.program_id(0)
    j = pl.program_id(1)
    is_lat = i < n_lat_tiles

    def for_source(fn):
        pl.when(is_lat)(lambda: fn(xl_ref))
        pl.when(jnp.logical_not(is_lat))(lambda: fn(xc_ref))

    @pl.when(j == 0)
    def _():
        def prologue(x_ref):
            h_ref[...] = _mod_norm(x_ref[...], nw_ref[...], m_ref[0:1, :], m_ref[1:2, :]).astype(BF16)
        for_source(prologue)
        acc_ref[...] = jnp.zeros_like(acc_ref)
        _conv_masks(mask_ref, jnp.where(is_lat, GRID_W, ctx_row))

    h = h_ref[...]
    gate_b = _dot(h, wb_ref[...])
    u = _dot(h, wc_ref[...]) * _dot(h, wv_ref[...])
    g = (gate_b * _dwconv_rows(u, cw_ref, mask_ref, 0)).astype(BF16)
    acc_ref[...] += _dot(g, wo_ref[...])

    @pl.when(j == pl.num_programs(1) - 1)
    def _():
        def epilogue(x_ref):
            o_ref[...] = x_ref[...] + m_ref[2:3, :] * acc_ref[...]
        for_source(epilogue)


def _short_conv_layer(x_lat, x_ctx, mods, layer, nw, w_in, conv_w, w_out, *, seq, ctx_row):
    t_lat, d = x_lat.shape
    t_ctx = x_ctx.shape[0]
    tm, tc = TOKEN_TILE, min(COL_TILE, d)
    n_lat, n_ctx, nj = t_lat // tm, t_ctx // tm, d // tc
    mod_row = _mod_row_fn(mods, seq, tm)
    kern = functools.partial(_sc_kernel, n_lat_tiles=n_lat, ctx_row=ctx_row)
    return pl.pallas_call(
        kern,
        grid=(n_lat + n_ctx, nj),
        in_specs=[
            pl.BlockSpec((tm, d), lambda i, j: (jnp.minimum(i, n_lat - 1), 0)),
            pl.BlockSpec((tm, d), lambda i, j: (jnp.maximum(i - n_lat, 0), 0)),
            pl.BlockSpec((None, None, 6, d), lambda i, j: (layer, mod_row(i), 0, 0)),
            pl.BlockSpec((1, d), lambda i, j: (0, 0)),
            pl.BlockSpec((d, tc), lambda i, j: (0, j)),
            pl.BlockSpec((d, tc), lambda i, j: (0, nj + j)),
            pl.BlockSpec((d, tc), lambda i, j: (0, 2 * nj + j)),
            pl.BlockSpec((conv_w.shape[0], tc), lambda i, j: (0, j)),
            pl.BlockSpec((tc, d), lambda i, j: (j, 0)),
        ],
        out_specs=pl.BlockSpec((tm, d), lambda i, j: (i, 0)),
        out_shape=jax.ShapeDtypeStruct((t_lat + t_ctx, d), F32),
        scratch_shapes=[pltpu.VMEM((tm, d), BF16), pltpu.VMEM((tm, d), F32),
                        pltpu.VMEM((conv_w.shape[0] - 1, tm, LANES), F32)],
        compiler_params=_params("arbitrary", "arbitrary"),
        name="short_conv_mixer",
    )(x_lat, x_ctx, mods, nw, w_in, w_in, w_in, conv_w, w_out)


def _route_kernel(x_ref, m_ref, nw_ref, wr_ref, br_ref, fp_ref, meta_ref, wts_ref, cnt_ref, run_ref,
                  *, n_groups, per_group):
    i = pl.program_id(0)

    @pl.when(i == 0)
    def _():
        run_ref[...] = jnp.zeros_like(run_ref)

    f = _mod_norm(x_ref[...], nw_ref[...], m_ref[3:4, :], m_ref[4:5, :])
    _store_token_tiles(fp_ref, _pack_row(f))
    logits = _dot(f.astype(BF16), wr_ref[...]) + br_ref[...]
    tm = logits.shape[0]
    lane = lax.broadcasted_iota(I32, logits.shape, 1).astype(F32)
    neg = -jnp.inf

    g_mask = lane < n_groups
    gl = jnp.where(g_mask, logits, neg)
    g_max = jnp.max(gl, axis=1, keepdims=True)
    g_sel = jnp.min(jnp.where(gl == g_max, lane, LANES), axis=1, keepdims=True)
    p_group = 1.0 / jnp.sum(jnp.where(g_mask, jnp.exp(gl - g_max), 0.0), axis=1, keepdims=True)

    e_lo = n_groups + g_sel * per_group
    in_group = (lane >= e_lo) & (lane < e_lo + per_group)
    el = jnp.where(in_group, logits, neg)
    e_max = jnp.max(el, axis=1, keepdims=True)
    ex = jnp.where(in_group, jnp.exp(el - e_max), 0.0)
    p = jnp.where(in_group, ex / jnp.sum(ex, axis=1, keepdims=True), -1.0)
    p1 = jnp.max(p, axis=1, keepdims=True)
    l1 = jnp.min(jnp.where(p == p1, lane, LANES), axis=1, keepdims=True)
    p_rest = jnp.where(lane == l1, -1.0, p)
    p2 = jnp.max(p_rest, axis=1, keepdims=True)
    l2 = jnp.min(jnp.where(p_rest == p2, lane, LANES), axis=1, keepdims=True)
    denom = p1 + p2
    w1 = p1 / denom * p_group
    w2 = p2 / denom * p_group

    hit1 = lane == l1
    hit2 = lane == l2
    onehot = jnp.where(hit1 | hit2, 1.0, 0.0)
    earlier = lax.broadcasted_iota(I32, (tm, tm), 0) > lax.broadcasted_iota(I32, (tm, tm), 1)
    before = _dot(jnp.where(earlier, 1.0, 0.0).astype(BF16), onehot.astype(BF16)) + run_ref[...]
    r1 = jnp.sum(jnp.where(hit1, before, 0.0), axis=1, keepdims=True)
    r2 = jnp.sum(jnp.where(hit2, before, 0.0), axis=1, keepdims=True)
    run_ref[...] += jnp.sum(onehot, axis=0, keepdims=True)
    cnt_ref[...] = run_ref[...].astype(I32)

    meta = jnp.where(lane == 0, l1 - n_groups,
                     jnp.where(lane == 1, l2 - n_groups, jnp.where(lane == 2, r1, r2)))
    meta_ref[...] = meta.T[0:META_ROWS, :].astype(I32)
    wts_ref[...] = jnp.where(lane == 0, w1, w2)


def _moe_route(tokens, n_tok, mods, layer, nw, wr, br, *, seq, n_groups, per_group):
    d = tokens.shape[1]
    tm = TOKEN_TILE
    s = d // 2 // LANES
    mod_row = _mod_row_fn(mods, seq, tm)
    kern = functools.partial(_route_kernel, n_groups=n_groups, per_group=per_group)
    return pl.pallas_call(
        kern,
        grid=(n_tok // tm,),
        in_specs=[
            pl.BlockSpec((tm, d), lambda i: (i, 0)),
            pl.BlockSpec((None, None, 6, d), lambda i: (layer, mod_row(i), 0, 0)),
            pl.BlockSpec((1, d), lambda i: (0, 0)),
            pl.BlockSpec((d, LANES), lambda i: (0, 0)),
            pl.BlockSpec((1, LANES), lambda i: (0, 0)),
        ],
        out_specs=[
            pl.BlockSpec((tm * s, LANES), lambda i: (i, 0)),
            pl.BlockSpec((META_ROWS, tm), lambda i: (i, 0)),
            pl.BlockSpec((tm, LANES), lambda i: (i, 0)),
            pl.BlockSpec((1, LANES), lambda i: (0, 0)),
        ],
        out_shape=[
            jax.ShapeDtypeStruct((n_tok * s, LANES), U32),
            jax.ShapeDtypeStruct((n_tok // tm * META_ROWS, tm), I32),
            jax.ShapeDtypeStruct((n_tok, LANES), F32),
            jax.ShapeDtypeStruct((1, LANES), I32),
        ],
        scratch_shapes=[pltpu.VMEM((1, LANES), F32)],
        compiler_params=_params("arbitrary"),
        name="moe_route",
    )(tokens, mods, nw, wr, br)


def _route_plan(counts, *, n_tok, n_exp, n_groups, tg):
    cnt = counts[0, n_groups:n_groups + n_exp]
    ends = jnp.cumsum(cnt)
    offs = ends - cnt
    n_row_tiles = TOP_K * n_tok // tg
    first_tile = offs // tg
    n_it = jnp.where(cnt > 0, (ends - 1) // tg - first_tile + 1, 0)
    it_end = jnp.cumsum(n_it)
    it_start = it_end - n_it
    n_items = it_end[-1]
    w = jnp.minimum(jnp.arange(n_row_tiles + n_exp - 1, dtype=I32), n_items - 1)
    item_exp = jnp.sum(it_end[None, :] <= w[:, None], axis=1).astype(I32)
    item_tile = (first_tile[item_exp] + w - it_start[item_exp]).astype(I32)
    row_lo = jnp.maximum(offs[item_exp] - item_tile * tg, 0).astype(I32)
    row_hi = jnp.minimum(ends[item_exp] - item_tile * tg, tg).astype(I32)
    first = jnp.concatenate([jnp.ones((1,), I32), (item_tile[1:] != item_tile[:-1]).astype(I32)])
    return offs.astype(I32), (item_exp, item_tile, row_lo, row_hi, first, n_items.astype(I32).reshape(1))


PERM_SLOTS = 3


def _token_copy(src, src_tok, dst, dst_tok, s, sem):
    return pltpu.make_async_copy(src.at[pl.ds(pl.multiple_of(src_tok * s, s), s), :],
                                 dst.at[pl.ds(pl.multiple_of(dst_tok * s, s), s), :], sem)


def _sorted_row(offs_ref, meta_ref, k, r):
    return offs_ref[meta_ref[0, k, r]] + meta_ref[0, TOP_K + k, r]


def _sort_kernel(offs_ref, pos_ref, src_hbm, dst_hbm, buf, lsem, rsem, *, s):
    i = pl.program_id(0)
    n = pl.num_programs(0)
    tp = pos_ref.shape[2]
    rows = tp * s
    slot = i % PERM_SLOTS

    def tile_load(step, sl):
        return pltpu.make_async_copy(src_hbm.at[pl.ds(pl.multiple_of(step * rows, rows), rows), :], buf.at[sl],
                                     lsem.at[sl])

    def wait_rows(sl):
        for _ in range(TOP_K):
            pltpu.make_async_copy(buf.at[sl], dst_hbm.at[pl.ds(0, rows), :], rsem.at[sl]).wait()

    @pl.when(i == 0)
    def _():
        tile_load(0, 0).start()

    @pl.when(i + 1 < n)
    def _():
        tile_load(i + 1, (i + 1) % PERM_SLOTS).start()

    tile_load(i, slot).wait()

    def body(r, c):
        for k in range(TOP_K):
            _token_copy(buf.at[slot], r, dst_hbm, _sorted_row(offs_ref, pos_ref, k, r), s, rsem.at[slot]).start()
        return c
    lax.fori_loop(0, tp, body, 0, unroll=8)

    @pl.when(i >= 1)
    def _():
        wait_rows((i - 1) % PERM_SLOTS)

    @pl.when(i == n - 1)
    def _():
        wait_rows(slot)


def _expert_rows_for_tile(i, n, offs_ref, pos_ref, posn_ref, ys_hbm, ybuf, sem, s):
    slot = i % 2

    def request(p_ref, sl):
        tp = p_ref.shape[2]

        def body(r, c):
            for k in range(TOP_K):
                _token_copy(ys_hbm, _sorted_row(offs_ref, p_ref, k, r), ybuf.at[sl, k], r, s, sem.at[sl]).start()
            return c
        lax.fori_loop(0, tp, body, 0, unroll=8)

    @pl.when(i == 0)
    def _():
        request(pos_ref, 0)

    @pl.when(i + 1 < n)
    def _():
        request(posn_ref, 1 - slot)

    for k in range(TOP_K):
        pltpu.make_async_copy(ys_hbm.at[pl.ds(0, ybuf.shape[2]), :], ybuf.at[slot, k], sem.at[slot]).wait()
    return slot


def _unsort_kernel(offs_ref, pos_ref, src_hbm, dst_hbm, buf, rsem, wsem, *, s, n_blocks):
    i = pl.program_id(0)
    tp = pos_ref.shape[2]
    rows = tp * s
    slot = i % PERM_SLOTS

    def write_back(step, sl, k):
        return pltpu.make_async_copy(buf.at[sl, k], dst_hbm.at[k, pl.ds(pl.multiple_of(step * rows, rows), rows), :],
                                     wsem.at[sl])

    def wait_rows(sl):
        for k in range(TOP_K):
            pltpu.make_async_copy(src_hbm.at[pl.ds(0, rows), :], buf.at[sl, k], rsem.at[sl]).wait()

    def finish(step, sl):
        wait_rows(sl)
        for k in range(TOP_K):
            write_back(step, sl, k).start()

    def wait_write_back(step, sl):
        for k in range(TOP_K):
            write_back(step, sl, k).wait()

    @pl.when(i >= PERM_SLOTS)
    def _():
        wait_write_back(i - PERM_SLOTS, slot)

    def body(r, c):
        for k in range(TOP_K):
            _token_copy(src_hbm, _sorted_row(offs_ref, pos_ref, k, r), buf.at[slot, k], r, s, rsem.at[slot]).start()
        return c
    lax.fori_loop(0, tp, body, 0, unroll=8)

    @pl.when(i >= 1)
    def _():
        finish(i - 1, (i - 1) % PERM_SLOTS)

    @pl.when(i == n_blocks - 1)
    def _():
        finish(i, slot)
        for back in range(min(PERM_SLOTS, n_blocks)):
            wait_write_back(i - back, (i - back) % PERM_SLOTS)


def _unsort_rows(offs, meta_blocks, src, *, n_tok, s):
    n_blocks, _, tp = meta_blocks.shape
    return pl.pallas_call(
        functools.partial(_unsort_kernel, s=s, n_blocks=n_blocks),
        grid=(n_blocks,),
        in_specs=[pl.BlockSpec(memory_space=pltpu.SMEM),
                  pl.BlockSpec((1, META_ROWS, tp), lambda i: (i, 0, 0), memory_space=pltpu.SMEM),
                  pl.BlockSpec(memory_space=pl.ANY)],
        out_specs=pl.BlockSpec(memory_space=pl.ANY),
        out_shape=jax.ShapeDtypeStruct((TOP_K, n_tok * s, LANES), src.dtype),
        scratch_shapes=[pltpu.VMEM((PERM_SLOTS, TOP_K, tp * s, LANES), src.dtype),
                        pltpu.SemaphoreType.DMA((PERM_SLOTS,)), pltpu.SemaphoreType.DMA((PERM_SLOTS,))],
        compiler_params=_params("arbitrary"),
        name="moe_unsort_rows",
    )(offs, meta_blocks, src)


def _pos_specs(n_blocks, tp):
    return [pl.BlockSpec(memory_space=pltpu.SMEM),
            pl.BlockSpec((1, META_ROWS, tp), lambda i: (i, 0, 0), memory_space=pltpu.SMEM),
            pl.BlockSpec((1, META_ROWS, tp), lambda i: (jnp.minimum(i + 1, n_blocks - 1), 0, 0),
                         memory_space=pltpu.SMEM)]


def _sort_rows(offs, meta_blocks, src, *, n_tok, s):
    n_blocks, _, tp = meta_blocks.shape
    return pl.pallas_call(
        functools.partial(_sort_kernel, s=s),
        grid=(n_blocks,),
        in_specs=[pl.BlockSpec(memory_space=pltpu.SMEM),
                  pl.BlockSpec((1, META_ROWS, tp), lambda i: (i, 0, 0), memory_space=pltpu.SMEM),
                  pl.BlockSpec(memory_space=pl.ANY)],
        out_specs=pl.BlockSpec(memory_space=pl.ANY),
        out_shape=jax.ShapeDtypeStruct((TOP_K * n_tok * s, LANES), src.dtype),
        scratch_shapes=[pltpu.VMEM((PERM_SLOTS, tp * s, LANES), src.dtype),
                        pltpu.SemaphoreType.DMA((PERM_SLOTS,)), pltpu.SemaphoreType.DMA((PERM_SLOTS,))],
        compiler_params=_params("arbitrary"),
        name="moe_sort_rows",
    )(offs, meta_blocks, src)


def _gmm_kernel(ie_ref, it_ref, lo_ref, hi_ref, first_ref, n_ref, xs_ref, wg_ref, wu_ref, wd_ref, ys_ref,
                wg_b, wu_b, wd_b):
    w = pl.program_id(0)

    @pl.when(w < n_ref[0])
    def _():
        @pl.when((w == 0) | (ie_ref[w] != ie_ref[jnp.maximum(w - 1, 0)]))
        def _():
            wg_b[...] = wg_ref[...].astype(BF16)
            wu_b[...] = wu_ref[...].astype(BF16)
            wd_b[...] = wd_ref[...].astype(BF16)

        half = wg_b.shape[0] // 2
        s = half // LANES
        x_lo, x_hi = _unpack_pair(_load_token_tiles(xs_ref, s))
        x_lo, x_hi = x_lo.astype(BF16), x_hi.astype(BF16)
        hg = _dot(x_lo, wg_b[0:half, :]) + _dot(x_hi, wg_b[half:, :])
        hu = _dot(x_lo, wu_b[0:half, :]) + _dot(x_hi, wu_b[half:, :])
        yp = _pack_row(_dot((_silu(hg) * hu).astype(BF16), wd_b[...]))
        rows = lax.broadcasted_iota(I32, yp.shape, 0)
        mine = (rows >= lo_ref[w]) & (rows < hi_ref[w])

        @pl.when(first_ref[w] == 1)
        def _():
            _store_token_tiles(ys_ref, jnp.where(mine, yp, jnp.uint32(0)))

        @pl.when(first_ref[w] == 0)
        def _():
            _store_token_tiles(ys_ref, jnp.where(mine, yp, _load_token_tiles(ys_ref, s)))


def _moe_experts(xs, plan, layer, wg, wu, wd):
    _, _, d, de = wg.shape
    s = d // 2 // LANES
    tg = GROUP_TILE
    n_items_max = plan[0].shape[0]
    grid_spec = pltpu.PrefetchScalarGridSpec(
        num_scalar_prefetch=6,
        grid=(n_items_max,),
        in_specs=[
            pl.BlockSpec((tg * s, LANES), lambda w, ie, it, lo, hi, fi, n: (it[w], 0)),
            pl.BlockSpec((None, None, d, de), lambda w, ie, it, lo, hi, fi, n: (layer, ie[w], 0, 0)),
            pl.BlockSpec((None, None, d, de), lambda w, ie, it, lo, hi, fi, n: (layer, ie[w], 0, 0)),
            pl.BlockSpec((None, None, de, d), lambda w, ie, it, lo, hi, fi, n: (layer, ie[w], 0, 0)),
        ],
        out_specs=pl.BlockSpec((tg * s, LANES), lambda w, ie, it, lo, hi, fi, n: (it[w], 0)),
        scratch_shapes=[pltpu.VMEM((d, de), BF16), pltpu.VMEM((d, de), BF16), pltpu.VMEM((de, d), BF16)],
    )
    return pl.pallas_call(
        _gmm_kernel,
        grid_spec=grid_spec,
        out_shape=jax.ShapeDtypeStruct(xs.shape, U32),
        compiler_params=_params("arbitrary"),
        name="moe_experts",
    )(*plan, xs, wg, wu, wd)


def _moe_combine(y0_ref, y1_ref, wts_ref, s):
    wts = wts_ref[...]
    return (wts[:, 0:1] * _unpack_row(_load_token_tiles(y0_ref, s))
            + wts[:, 1:2] * _unpack_row(_load_token_tiles(y1_ref, s)))


def _inproj_kernel(x_ref, y0_ref, y1_ref, wts_ref, mp_ref, m_ref, nw_ref, w_ref, wdt_ref, cw_ref, cb_ref, dtb_ref,
                   lat_ref, z_ref, xbc_ref, dt_ref, h_ref, mask_ref, *, nz, n_lat_tiles, ctx_row):
    i = pl.program_id(0)
    j = pl.program_id(1)
    n_sub = w_ref.shape[1] // COL_TILE

    @pl.when(j == 0)
    def _():
        s = x_ref.shape[1] // 2 // LANES
        lat = x_ref[...] + mp_ref[5:6, :] * _moe_combine(y0_ref, y1_ref, wts_ref, s)
        lat_ref[...] = lat
        h_ref[...] = _mod_norm(lat, nw_ref[...], m_ref[0:1, :], m_ref[1:2, :]).astype(BF16)
        _conv_masks(mask_ref, jnp.where(i < n_lat_tiles, GRID_W, ctx_row))

    @pl.when(j < nz)
    def _():
        for s in range(n_sub):
            sl = slice(s * COL_TILE, (s + 1) * COL_TILE)
            z_ref[:, sl] = _dot(h_ref[...], w_ref[:, sl]).astype(BF16)

    def project_conv(row_len):
        for s in range(n_sub):
            sl = slice(s * COL_TILE, (s + 1) * COL_TILE)
            r = _dot(h_ref[...], w_ref[:, sl])
            xbc_ref[:, sl] = _silu(_dwconv_rows_static(r, cw_ref, s * COL_TILE, row_len) + cb_ref[:, sl]).astype(BF16)

    pl.when((j >= nz) & (i < n_lat_tiles))(lambda: project_conv(GRID_W))
    pl.when((j >= nz) & (i >= n_lat_tiles))(lambda: project_conv(ctx_row))

    @pl.when(j == pl.num_programs(1) - 1)
    def _():
        v = _dot(h_ref[...], wdt_ref[...]) + dtb_ref[...]
        dt_ref[...] = jnp.maximum(v, 0.0) + jnp.log1p(jnp.exp(-jnp.abs(v)))


def _ssd_in_proj(tokens, y2, wts, mods, layer, nw, w_zx, w_dt, conv_w, conv_b, dt_bias,
                 *, seq, n_lat_tiles, ctx_row, d_inner, conv_dim):
    t, d = tokens.shape
    tm = TOKEN_TILE
    tc = next(c for c in (4 * COL_TILE, 2 * COL_TILE, COL_TILE) if d_inner % c == 0 and conv_dim % c == 0)
    nz, nx = d_inner // tc, conv_dim // tc
    mod_row = _mod_row_fn(mods, seq, tm)

    def xcol(j):
        return jnp.maximum(j - nz, 0)

    kern = functools.partial(_inproj_kernel, nz=nz, n_lat_tiles=n_lat_tiles, ctx_row=ctx_row)
    return pl.pallas_call(
        kern,
        grid=(t // tm, nz + nx),
        in_specs=[
            pl.BlockSpec((tm, d), lambda i, j: (i, 0)),
            pl.BlockSpec((None, tm * (d // 2 // LANES), LANES), lambda i, j: (0, i, 0)),
            pl.BlockSpec((None, tm * (d // 2 // LANES), LANES), lambda i, j: (1, i, 0)),
            pl.BlockSpec((tm, LANES), lambda i, j: (i, 0)),
            pl.BlockSpec((None, None, 6, d), lambda i, j: (layer - 1, mod_row(i), 0, 0)),
            pl.BlockSpec((None, None, 6, d), lambda i, j: (layer, mod_row(i), 0, 0)),
            pl.BlockSpec((1, d), lambda i, j: (0, 0)),
            pl.BlockSpec((d, tc), lambda i, j: (0, j)),
            pl.BlockSpec((d, LANES), lambda i, j: (0, 0)),
            pl.BlockSpec((conv_w.shape[0], tc), lambda i, j: (0, xcol(j))),
            pl.BlockSpec((1, tc), lambda i, j: (0, xcol(j))),
            pl.BlockSpec((1, LANES), lambda i, j: (0, 0)),
        ],
        out_specs=[
            pl.BlockSpec((tm, d), lambda i, j: (i, 0)),
            pl.BlockSpec((tm, tc), lambda i, j: (i, jnp.minimum(j, nz - 1))),
            pl.BlockSpec((tm, tc), lambda i, j: (i, xcol(j))),
            pl.BlockSpec((tm, LANES), lambda i, j: (i, 0)),
        ],
        out_shape=[
            jax.ShapeDtypeStruct((t, d), F32),
            jax.ShapeDtypeStruct((t, d_inner), BF16),
            jax.ShapeDtypeStruct((t, conv_dim), BF16),
            jax.ShapeDtypeStruct((t, LANES), F32),
        ],
        scratch_shapes=[pltpu.VMEM((tm, d), BF16), pltpu.VMEM((conv_w.shape[0] - 1, tm, LANES), F32)],
        compiler_params=_params("arbitrary", "arbitrary"),
        name="ssd_in_proj",
    )(tokens, y2, y2, wts, mods, mods, nw, w_zx, w_dt, conv_w, conv_b, dt_bias)


def _ssd_kernel(x_ref, b_ref, c_ref, dt_ref, a_ref, y_ref, st_ref, *, n_heads):
    d = pl.program_id(1)
    k = pl.program_id(2)
    q = x_ref.shape[0]
    n_groups = st_ref.shape[0]
    gw = st_ref.shape[2]
    hpg = gw // HEAD_DIM
    pw = 2 * HEAD_DIM
    fwd = d == 0

    @pl.when(k == 0)
    def _():
        st_ref[...] = jnp.zeros_like(st_ref)

    t_i = lax.broadcasted_iota(I32, (q, q), 0)
    s_i = lax.broadcasted_iota(I32, (q, q), 1)
    incl = jnp.where(fwd, t_i - s_i, s_i - t_i) >= 0

    dt_all = dt_ref[...]
    dta_all = dt_all * (a_ref[...] * LOG2_E)
    cum_all = jnp.dot(jnp.where(incl, 1.0, 0.0), dta_all, preferred_element_type=F32,
                      precision=lax.Precision.HIGHEST)
    cum_t_all = cum_all.T
    dt_t_all = dt_all.T
    tot_t_all = jnp.sum(dta_all.T, axis=1, keepdims=True)

    def pick_cols(a):
        return jnp.where(fwd, a[:, 0:n_heads], a[:, n_heads:2 * n_heads])

    def pick_rows(a):
        return jnp.where(fwd, a[0:n_heads, :], a[n_heads:2 * n_heads, :])

    cum = pick_cols(cum_all)
    tot = pick_cols(jnp.sum(dta_all, axis=0, keepdims=True))
    cum_t = pick_rows(cum_t_all)
    dt_t = pick_rows(dt_t_all)
    to_end_t = (dt_t * jnp.exp2(pick_rows(tot_t_all) - cum_t)).astype(BF16)
    dt_tb = dt_t.astype(BF16)
    dec_all = jnp.exp2(tot)

    low_half = lax.broadcasted_iota(I32, (q, pw), 1) < HEAD_DIM
    low_row = lax.broadcasted_iota(I32, (1, pw), 1) < HEAD_DIM

    for g in range(n_groups):
        bg = b_ref[:, g * D_STATE:(g + 1) * D_STATE]
        cg = c_ref[:, g * D_STATE:(g + 1) * D_STATE]
        cb = lax.dot_general(cg, bg, (((1,), (1,)), ((), ())), preferred_element_type=F32)
        cb = jnp.where(incl, cb, 0.0).astype(BF16)
        bg_t = bg.astype(F32).T.astype(BF16)
        st = st_ref[g]
        y_off = _dot(cg, st.astype(BF16))

        for pair in range(hpg // 2):
            psl = slice(pair * pw, (pair + 1) * pw)
            xp = x_ref[:, g * gw + pair * pw:g * gw + (pair + 1) * pw]
            zero = jnp.zeros_like(xp)
            rhs = jnp.concatenate([jnp.where(low_half, xp, zero), jnp.where(low_half, zero, xp)], axis=0)
            w_parts, u_parts, cols = [], [], []
            for hh in range(2):
                h = g * hpg + pair * 2 + hh
                col = jnp.broadcast_to(cum[:, h:h + 1], (q, q))
                seg = jnp.exp2(jnp.minimum(col - cum_t[h:h + 1, :], 0.0))
                w_parts.append(cb * seg.astype(BF16) * dt_tb[h:h + 1, :])
                u_parts.append(bg_t * to_end_t[h:h + 1, :])
                cols.append(col)
            y_diag = _dot(jnp.concatenate(w_parts, axis=1), rhs)
            upd = _dot(jnp.concatenate(u_parts, axis=1), rhs)
            dec_in = jnp.exp2(jnp.where(low_half, cols[0], cols[1]))
            y_ref[:, g * gw + pair * pw:g * gw + (pair + 1) * pw] = (y_diag + y_off[:, psl] * dec_in).astype(BF16)
            h0 = g * hpg + pair * 2
            dec_pair = jnp.where(low_row, jnp.broadcast_to(dec_all[:, h0:h0 + 1], (1, pw)),
                                 jnp.broadcast_to(dec_all[:, h0 + 1:h0 + 2], (1, pw)))
            st_ref[g, :, psl] = st[:, psl] * dec_pair + upd


def _ssd_scan(xbc, dt, a_neg, *, n_batch, lat_chunks, ctx_chunks, d_inner, n_heads):
    t = xbc.shape[0]
    q = CHUNK
    gn = SSD_GROUPS * D_STATE
    gw = d_inner // SSD_GROUPS
    assert gw % (2 * HEAD_DIM) == 0 and d_inner % gn == 0 and q == 2 * HEAD_DIM and q == LANES
    n_steps = ctx_chunks + lat_chunks
    ctx_base = n_batch * lat_chunks

    def chunk(b, d, k):
        kc = jnp.where(d == 0, k, ctx_chunks - 1 - k)
        kl = jnp.where(d == 0, k - ctx_chunks, n_steps - 1 - k)
        return jnp.where(k < ctx_chunks, ctx_base + b * ctx_chunks + kc, b * lat_chunks + kl)

    kern = functools.partial(_ssd_kernel, n_heads=n_heads)
    return pl.pallas_call(
        kern,
        grid=(n_batch, 2, n_steps),
        in_specs=[
            pl.BlockSpec((q, d_inner), lambda b, d, k: (chunk(b, d, k), 0)),
            pl.BlockSpec((q, gn), lambda b, d, k: (chunk(b, d, k), d_inner // gn)),
            pl.BlockSpec((q, gn), lambda b, d, k: (chunk(b, d, k), d_inner // gn + 1)),
            pl.BlockSpec((q, LANES), lambda b, d, k: (chunk(b, d, k), 0)),
            pl.BlockSpec((1, LANES), lambda b, d, k: (0, 0)),
        ],
        out_specs=pl.BlockSpec((None, q, d_inner), lambda b, d, k: (d, chunk(b, d, k), 0)),
        out_shape=jax.ShapeDtypeStruct((2, t, d_inner), BF16),
        scratch_shapes=[pltpu.VMEM((SSD_GROUPS, D_STATE, gw), F32)],
        compiler_params=_params("arbitrary", "arbitrary", "arbitrary"),
        name="ssd_scan",
    )(xbc, xbc, xbc, dt, a_neg)


def _outproj_kernel(yf_ref, yb_ref, xs_ref, z_ref, dsk_ref, gnw_ref, lat_ref, m_ref, w_ref, o_ref):
    gw = w_ref.shape[0] // SSD_GROUPS
    acc = None
    for g in range(SSD_GROUPS):
        sl = slice(g * gw, (g + 1) * gw)
        y = (yf_ref[:, sl].astype(F32) + yb_ref[:, sl].astype(F32)
             + dsk_ref[:, sl] * xs_ref[:, sl].astype(F32))
        gated = y * _silu(z_ref[:, sl].astype(F32))
        ms = jnp.mean(gated * gated, axis=-1, keepdims=True)
        gn = (gated * lax.rsqrt(ms + EPS) * gnw_ref[:, sl]).astype(BF16)
        part = _dot(gn, w_ref[sl, :])
        acc = part if acc is None else acc + part
    o_ref[...] = lat_ref[...] + m_ref[2:3, :] * acc


def _ssd_out_proj(y, xbc, z, d_skip, gnw, tokens, mods, layer, w_out, *, n_tok, seq):
    d_inner, d = w_out.shape
    tm = OUT_TOKEN_TILE
    mod_row = _mod_row_fn(mods, seq, tm)
    return pl.pallas_call(
        _outproj_kernel,
        grid=(n_tok // tm,),
        in_specs=[
            pl.BlockSpec((None, tm, d_inner), lambda i: (0, i, 0)),
            pl.BlockSpec((None, tm, d_inner), lambda i: (1, i, 0)),
            pl.BlockSpec((tm, d_inner), lambda i: (i, 0)),
            pl.BlockSpec((tm, d_inner), lambda i: (i, 0)),
            pl.BlockSpec((1, d_inner), lambda i: (0, 0)),
            pl.BlockSpec((1, d_inner), lambda i: (0, 0)),
            pl.BlockSpec((tm, d), lambda i: (i, 0)),
            pl.BlockSpec((None, None, 6, d), lambda i: (layer, mod_row(i), 0, 0)),
            pl.BlockSpec((d_inner, d), lambda i: (0, 0), pipeline_mode=pl.Buffered(1)),
        ],
        out_specs=pl.BlockSpec((tm, d), lambda i: (i, 0)),
        out_shape=jax.ShapeDtypeStruct((n_tok, d), F32),
        compiler_params=_params("arbitrary"),
        name="ssd_out_proj",
    )(y, y, xbc, z, d_skip, gnw, tokens, mods, w_out)


def _final_kernel(offs_ref, pos_ref, posn_ref, x_ref, ys_hbm, wts_ref, m_ref, nw_ref, o_ref, ybuf, ysem):
    s = x_ref.shape[1] // 2 // LANES
    slot = _expert_rows_for_tile(pl.program_id(0), pl.num_programs(0), offs_ref, pos_ref, posn_ref, ys_hbm, ybuf,
                                 ysem, s)
    lat = x_ref[...] + m_ref[5:6, :] * _moe_combine(ybuf.at[slot, 0], ybuf.at[slot, 1], wts_ref, s)
    ms = jnp.mean(lat * lat, axis=-1, keepdims=True)
    o_ref[...] = lat * lax.rsqrt(ms + EPS) * nw_ref[...]


def _final_norm(tokens, ys, offs, meta_blocks, wts, mods, layer, nw, *, seq):
    t, d = tokens.shape
    tm = TOKEN_TILE
    s = d // 2 // LANES
    assert meta_blocks.shape == (t // tm, META_ROWS, tm)
    mod_row = _mod_row_fn(mods, seq, tm)
    return pl.pallas_call(
        _final_kernel,
        grid=(t // tm,),
        in_specs=_pos_specs(t // tm, tm) + [
            pl.BlockSpec((tm, d), lambda i: (i, 0)),
            pl.BlockSpec(memory_space=pl.ANY),
            pl.BlockSpec((tm, LANES), lambda i: (i, 0)),
            pl.BlockSpec((None, None, 6, d), lambda i: (layer, mod_row(i), 0, 0)),
            pl.BlockSpec((1, d), lambda i: (0, 0)),
        ],
        out_specs=pl.BlockSpec((tm, d), lambda i: (i, 0)),
        out_shape=jax.ShapeDtypeStruct((t, d), F32),
        scratch_shapes=[pltpu.VMEM((2, TOP_K, tm * s, LANES), U32), pltpu.SemaphoreType.DMA((2,))],
        compiler_params=_params("arbitrary"),
        name="final_norm",
    )(offs, meta_blocks, meta_blocks, tokens, ys, wts, mods, nw)


def _router_weights(rg_w, rg_b, re_w, re_b):
    d, n_groups = rg_w.shape
    n_exp = re_w.shape[1]
    pad = LANES - n_groups - n_exp
    wr = jnp.concatenate([rg_w, re_w, jnp.zeros((d, pad), F32)], axis=1).astype(BF16)
    br = jnp.concatenate([rg_b, re_b, jnp.zeros((pad,), F32)]).reshape(1, LANES)
    return wr, br


def _moe(tokens, n_tok, mods, layer, nw, rg_w, rg_b, re_w, re_b, w_gate, w_up, w_down, *, seq):
    n_groups = rg_w.shape[1]
    n_exp = re_w.shape[1]
    s = tokens.shape[1] // 2 // LANES
    assert n_tok % PERM_TILE == 0 and (TOP_K * n_tok) % GROUP_TILE == 0 and tokens.shape[1] % (2 * LANES) == 0
    assert PERM_TILE == TOKEN_TILE
    wr, br = _router_weights(rg_w, rg_b, re_w, re_b)
    fp, meta, wts, counts = _moe_route(tokens, n_tok, mods, layer, nw, wr, br, seq=seq, n_groups=n_groups,
                                       per_group=n_exp // n_groups)
    offs, plan = _route_plan(counts, n_tok=n_tok, n_exp=n_exp, n_groups=n_groups, tg=GROUP_TILE)
    meta_blocks = meta.reshape(n_tok // PERM_TILE, META_ROWS, PERM_TILE)
    xs = _sort_rows(offs, meta_blocks, fp, n_tok=n_tok, s=s)
    return _moe_experts(xs, plan, layer, w_gate, w_up, w_down), offs, meta_blocks, wts


def kernel(x, c, ctx, c_ctx, ada_w, ada_b, norm_mix_w, norm_ffn_w, sc_w_in, sc_conv_w, sc_w_out, ssd_w_in,
           ssd_conv_w, ssd_conv_b, ssd_dt_bias, ssd_a_log, ssd_d, ssd_norm_w, ssd_w_out, rg_w, rg_b, re_w, re_b,
           moe_w_gate, moe_w_up, moe_w_down, final_norm_w):
    n_batch, seq, d = x.shape
    ctx_len = ctx.shape[1]
    depth = ada_w.shape[0]
    d_inner = ssd_w_out.shape[1]
    conv_dim = ssd_conv_w.shape[2]
    n_heads = ssd_d.shape[1]
    tm = TOKEN_TILE
    t_lat, t_ctx = n_batch * seq, n_batch * ctx_len
    assert depth == 2 and seq % tm == 0 and t_ctx % tm == 0 and tm % ctx_len == 0 and tm % GRID_W == 0
    assert ctx_len & (ctx_len - 1) == 0 and GRID_W & (GRID_W - 1) == 0
    assert seq % CHUNK == 0 and ctx_len % CHUNK == 0 and d_inner == n_heads * HEAD_DIM
    assert n_batch + 1 <= SUBLANES and 2 * n_heads <= LANES and d_inner % COL_TILE == 0 and conv_dim % COL_TILE == 0
    n_lat_tiles = t_lat // tm

    cvec = jnp.zeros((SUBLANES, d), F32).at[0:n_batch].set(c).at[n_batch].set(c_ctx)
    mods = _ada_mods(cvec, ada_w, ada_b)

    x_lat = x.reshape(t_lat, d)
    x_ctx = ctx.reshape(t_ctx, d)

    tokens = _short_conv_layer(x_lat, x_ctx, mods, 0, norm_mix_w[0:1], sc_w_in[0].astype(BF16), sc_conv_w[0],
                               sc_w_out[0].astype(BF16), seq=seq, ctx_row=ctx_len)
    t_all = t_lat + t_ctx
    ys, offs, meta_blocks, wts = _moe(tokens, t_all, mods, 0, norm_ffn_w[0:1], rg_w[0], rg_b[0], re_w[0], re_b[0],
                                      moe_w_gate, moe_w_up, moe_w_down, seq=seq)

    w_in = ssd_w_in[0]
    dt_cols = 2 * n_heads
    w_zx = w_in[:, :d_inner + conv_dim].astype(BF16)
    w_dt = jnp.concatenate([w_in[:, d_inner + conv_dim:], jnp.zeros((d, LANES - dt_cols), F32)],
                           axis=1).astype(BF16)
    dt_bias = jnp.zeros((1, LANES), F32).at[0, :dt_cols].set(ssd_dt_bias[0].reshape(-1))
    a_neg = jnp.zeros((1, LANES), F32).at[0, :dt_cols].set(-jnp.exp(ssd_a_log[0].reshape(-1)))
    y2 = _unsort_rows(offs, meta_blocks, ys, n_tok=t_all, s=d // 2 // LANES)
    tokens, z, xbc, dt = _ssd_in_proj(tokens, y2, wts, mods, 1, norm_mix_w[1:2], w_zx, w_dt, ssd_conv_w[0],
                                      ssd_conv_b[0].reshape(1, conv_dim), dt_bias, seq=seq,
                                      n_lat_tiles=n_lat_tiles, ctx_row=ctx_len, d_inner=d_inner, conv_dim=conv_dim)
    y = _ssd_scan(xbc, dt, a_neg, n_batch=n_batch, lat_chunks=seq // CHUNK, ctx_chunks=ctx_len // CHUNK,
                  d_inner=d_inner, n_heads=n_heads)
    d_skip = jnp.repeat(ssd_d[0], HEAD_DIM).reshape(1, d_inner)
    lat = _ssd_out_proj(y, xbc, z, d_skip, ssd_norm_w[0].reshape(1, d_inner), tokens, mods, 1,
                        ssd_w_out[0].astype(BF16), n_tok=t_lat, seq=seq)

    ys, offs, meta_blocks, wts = _moe(lat, t_lat, mods, 1, norm_ffn_w[1:2], rg_w[1], rg_b[1], re_w[1], re_b[1],
                                      moe_w_gate, moe_w_up, moe_w_down, seq=seq)
    out = _final_norm(lat, ys, offs, meta_blocks, wts, mods, 1, final_norm_w.reshape(1, d), seq=seq)
    return out.reshape(n_batch, seq, d)
```
